```python
import math
import jax
import jax.numpy as jnp
from jax import lax
import numpy as np


D_MODEL = 1024
BATCH = 4
SEQ = 4096
DEPTH = 1
DEC_BATCH = 32
DEC_SEQ = 1
PAST_LEN = 16384
PAGE_SIZE = 128

ATTN_WIDTH = D_MODEL // 2
CONV_CH = D_MODEL - ATTN_WIDTH
HEAD_DIM = 64
N_HEADS = ATTN_WIDTH // HEAD_DIM
IN_WIDTH = 3 * ATTN_WIDTH + 2 * CONV_CH
CONV_LEN = 31
MOBA_BLOCK = 256
MOBA_TOPK = 3
QUERY_BLOCK = 64
ROPE_THETA = 10000.0
N_EXPERT_GROUPS = 4
EXPERTS_PER_GROUP = 4
N_EXPERTS = N_EXPERT_GROUPS * EXPERTS_PER_GROUP
EXPERT_TOP_K = 2
D_EXPERT = D_MODEL // 2
RMS_EPS = 1e-6
LN_EPS = 1e-5
NEG_INF = -1e30

kernel_name = 'hymba_moba_conformer_hmoe_step'


def rms_norm(x, g):
    xf = x.astype(jnp.float32)
    y = xf * lax.rsqrt(jnp.mean(xf * xf, axis=-1, keepdims=True) + RMS_EPS)
    return (y * g.astype(jnp.float32)).astype(x.dtype)


def layer_norm(x, g, b):
    xf = x.astype(jnp.float32)
    mu = jnp.mean(xf, axis=-1, keepdims=True)
    var = jnp.mean(jnp.square(xf - mu), axis=-1, keepdims=True)
    y = (xf - mu) * lax.rsqrt(var + LN_EPS)
    return (y * g.astype(jnp.float32) + b.astype(jnp.float32)).astype(x.dtype)


def rope(x, pos):
    half = HEAD_DIM // 2
    inv_freq = jnp.exp(-math.log(ROPE_THETA) * jnp.arange(half, dtype=jnp.float32) / half)
    ang = pos.astype(jnp.float32)[:, None] * inv_freq[None, :]
    cos = jnp.cos(ang)[None, :, None, :]
    sin = jnp.sin(ang)[None, :, None, :]
    xf = x.astype(jnp.float32)
    x1, x2 = xf[..., :half], xf[..., half:]
    return jnp.concatenate([x1 * cos - x2 * sin, x2 * cos + x1 * sin], axis=-1).astype(x.dtype)


def in_projection(x, norm_g, w_in, b_in):
    bsz, slen, _ = x.shape
    z = rms_norm(x, norm_g) @ w_in + b_in
    a, c = ATTN_WIDTH, CONV_CH
    q, k, v, u_val, u_gate = jnp.split(z, [a, 2 * a, 3 * a, 3 * a + c], axis=-1)
    heads = lambda t: t.reshape(bsz, slen, N_HEADS, HEAD_DIM)
    u = u_val * jax.nn.sigmoid(u_gate)
    return heads(q), heads(k), heads(v), u


def conv_tail(u_hist, conv_w, conv_b, ln_g, ln_b):
    c = lax.conv_general_dilated(u_hist, conv_w[:, None, :], window_strides=(1,), padding='VALID',
                                 dimension_numbers=('NWC', 'WIO', 'NWC'),
                                 feature_group_count=u_hist.shape[-1]) + conv_b
    return jax.nn.silu(layer_norm(c, ln_g, ln_b))


def moba_attend(q, q_pos, kb, vb, k_mean):
    bsz, n_blocks = kb.shape[0], kb.shape[1]
    n_sel = min(MOBA_TOPK, n_blocks)
    own = q_pos // MOBA_BLOCK
    gate = jnp.einsum('bqhd,bnhd->bhqn', q.astype(jnp.float32), k_mean)
    fully_past = jnp.arange(n_blocks, dtype=jnp.int32)[None, :] < own[:, None]
    gate = jnp.where(fully_past, gate, NEG_INF)
    _, sel = lax.top_k(gate, n_sel)
    sel_ok = jnp.arange(n_sel, dtype=jnp.int32)[None, :] < own[:, None]
    own_idx = jnp.broadcast_to(own[None, None, :, None], sel.shape[:3] + (1,)).astype(sel.dtype)
    idx = jnp.concatenate([sel, own_idx], axis=-1)
    slot_ok = jnp.concatenate([sel_ok, jnp.ones_like(own[:, None], dtype=bool)], axis=-1)
    b_ix = jnp.arange(bsz)[:, None, None, None]
    h_ix = jnp.arange(N_HEADS)[None, :, None, None]
    kg = kb[b_ix, idx, :, h_ix]
    vg = vb[b_ix, idx, :, h_ix]
    kpos = idx[..., None] * MOBA_BLOCK + jnp.arange(MOBA_BLOCK, dtype=idx.dtype)
    mask = slot_ok[None, None, :, :, None] & (kpos <= q_pos[None, None, :, None, None])
    logits = jnp.einsum('bqhd,bhqsld->bhqsl', q, kg).astype(jnp.float32) * (HEAD_DIM ** -0.5)
    logits = jnp.where(mask, logits, NEG_INF)
    shp = logits.shape
    probs = jax.nn.softmax(logits.reshape(shp[:3] + (-1,)), axis=-1).reshape(shp)
    return jnp.einsum('bhqsl,bhqsld->bqhd', probs.astype(vg.dtype), vg)


def to_blocks(t, n_blocks):
    bsz, length = t.shape[0], t.shape[1]
    t = jnp.pad(t, ((0, 0), (0, n_blocks * MOBA_BLOCK - length), (0, 0), (0, 0)))
    return t.reshape(bsz, n_blocks, MOBA_BLOCK, N_HEADS, HEAD_DIM)


def moba_prompt(q, k, v, pos):
    bsz, slen = q.shape[0], q.shape[1]
    n_blocks = -(-slen // MOBA_BLOCK)
    kb, vb = to_blocks(k, n_blocks), to_blocks(v, n_blocks)
    k_mean = jnp.mean(kb.astype(jnp.float32), axis=2)
    n_q = slen // QUERY_BLOCK
    qc = q.reshape(bsz, n_q, QUERY_BLOCK, N_HEADS, HEAD_DIM).swapaxes(0, 1)
    pc = pos.reshape(n_q, QUERY_BLOCK)
    out = lax.map(lambda a: moba_attend(a[0], a[1], kb, vb, k_mean), (qc, pc))
    return out.swapaxes(0, 1).reshape(bsz, slen, ATTN_WIDTH)


def moba_sample(q, k, v, past_k, past_v, pos):
    bsz, sd = q.shape[0], q.shape[1]
    total = past_k.shape[1] + sd
    n_blocks = -(-total // MOBA_BLOCK)
    pad = jnp.zeros((bsz, n_blocks * MOBA_BLOCK - total, N_HEADS, HEAD_DIM), k.dtype)
    kb = jnp.concatenate([past_k, k, pad], axis=1).reshape(bsz, n_blocks, MOBA_BLOCK, N_HEADS, HEAD_DIM)
    vb = jnp.concatenate([past_v, v, pad.astype(v.dtype)], axis=1).reshape(bsz, n_blocks, MOBA_BLOCK, N_HEADS, HEAD_DIM)
    k_mean = jnp.mean(kb.astype(jnp.float32), axis=2)
    return moba_attend(q, pos, kb, vb, k_mean).reshape(bsz, sd, ATTN_WIDTH)


def hier_moe(h, w_group, b_group, w_router, b_router, w_gate, w_up, w_down):
    bsz, slen, d = h.shape
    t = h.reshape(-1, d)
    n_tok = t.shape[0]
    g_prob = jax.nn.softmax((t @ w_group).astype(jnp.float32) + b_group, axis=-1)
    g_sel = jnp.argmax(g_prob, axis=-1)
    g_w = jnp.take_along_axis(g_prob, g_sel[:, None], axis=-1)
    e_logits = ((t @ w_router).astype(jnp.float32) + b_router).reshape(n_tok, N_EXPERT_GROUPS, EXPERTS_PER_GROUP)
    e_in = jnp.take_along_axis(e_logits, g_sel[:, None, None], axis=1)[:, 0]
    top_p, top_i = lax.top_k(jax.nn.softmax(e_in, axis=-1), EXPERT_TOP_K)
    top_p = top_p / jnp.sum(top_p, axis=-1, keepdims=True)
    expert_id = g_sel[:, None] * EXPERTS_PER_GROUP + top_i
    combine = jnp.sum(jax.nn.one_hot(expert_id, N_EXPERTS, dtype=jnp.float32) * (g_w * top_p)[..., None], axis=1)
    hid = jax.nn.silu(jnp.einsum('td,edf->tef', t, w_gate)) * jnp.einsum('td,edf->tef', t, w_up)
    hid = hid * combine[:, :, None].astype(hid.dtype)
    return jnp.einsum('tef,efd->td', hid, w_down).reshape(bsz, slen, d)


def post_mixer(x, attn, conv, w_out, norm2_g, w_group, b_group, w_router, b_router, w_gate, w_up, w_down):
    x = x + jnp.concatenate([attn, conv], axis=-1) @ w_out
    return x + hier_moe(rms_norm(x, norm2_g), w_group, b_group, w_router, b_router, w_gate, w_up, w_down)


def setup_inputs(seed: int = 0) -> dict:
    key = jax.random.key(seed)
    ks = jax.random.split(key, 24)
    nrm = lambda k, shape, scale: jax.random.normal(k, shape, jnp.float32) * scale
    n_pages = PAST_LEN // PAGE_SIZE
    n_phys = (DEC_BATCH * n_pages * 5 + 3) // 4
    page_table = jax.random.permutation(ks[5], n_phys)[:DEC_BATCH * n_pages].reshape(DEC_BATCH, n_pages).astype(jnp.int32)
    return {
        'x_prompt': nrm(ks[0], (BATCH, SEQ, D_MODEL), 1.0),
        'x_sample': nrm(ks[1], (DEC_BATCH, DEC_SEQ, D_MODEL), 1.0),
        'cache_k': nrm(ks[2], (DEPTH, n_phys, PAGE_SIZE, N_HEADS, HEAD_DIM), 1.0),
        'cache_v': nrm(ks[3], (DEPTH, n_phys, PAGE_SIZE, N_HEADS, HEAD_DIM), 1.0),
        'state_conv': nrm(ks[4], (DEPTH, DEC_BATCH, CONV_LEN - 1, CONV_CH), 0.5),
        'page_table': page_table,
        'norm1_g': 1.0 + nrm(ks[6], (DEPTH, D_MODEL), 0.01),
        'w_in': nrm(ks[7], (DEPTH, D_MODEL, IN_WIDTH), D_MODEL ** -0.5),
        'b_in': nrm(ks[8], (DEPTH, IN_WIDTH), 0.01),
        'conv_w': nrm(ks[9], (DEPTH, CONV_LEN, CONV_CH), CONV_LEN ** -0.5),
        'conv_b': nrm(ks[10], (DEPTH, CONV_CH), 0.01),
        'conv_ln_g': 1.0 + nrm(ks[11], (DEPTH, CONV_CH), 0.01),
        'conv_ln_b': nrm(ks[12], (DEPTH, CONV_CH), 0.01),
        'w_out': nrm(ks[13], (DEPTH, D_MODEL, D_MODEL), D_MODEL ** -0.5),
        'norm2_g': 1.0 + nrm(ks[14], (DEPTH, D_MODEL), 0.01),
        'w_group': nrm(ks[15], (DEPTH, D_MODEL, N_EXPERT_GROUPS), D_MODEL ** -0.5),
        'b_group': nrm(ks[16], (DEPTH, N_EXPERT_GROUPS), 0.01),
        'w_router': nrm(ks[17], (DEPTH, D_MODEL, N_EXPERTS), D_MODEL ** -0.5),
        'b_router': nrm(ks[18], (DEPTH, N_EXPERTS), 0.01),
        'w_gate': nrm(ks[19], (DEPTH, N_EXPERTS, D_MODEL, D_EXPERT), D_MODEL ** -0.5),
        'w_up': nrm(ks[20], (DEPTH, N_EXPERTS, D_MODEL, D_EXPERT), D_MODEL ** -0.5),
        'w_down': nrm(ks[21], (DEPTH, N_EXPERTS, D_EXPERT, D_MODEL), D_EXPERT ** -0.5),
        'norm_f_g': 1.0 + nrm(ks[22], (D_MODEL,), 0.01),
    }


def reference(x_prompt, x_sample, cache_k, cache_v, state_conv, page_table, norm1_g, w_in, b_in,
              conv_w, conv_b, conv_ln_g, conv_ln_b, w_out, norm2_g, w_group, b_group, w_router,
              b_router, w_gate, w_up, w_down, norm_f_g):
    bsz, slen, _ = x_prompt.shape
    dbsz, dlen, _ = x_sample.shape
    past_len = page_table.shape[1] * cache_k.shape[2]
    pos_p = jnp.arange(slen, dtype=jnp.int32)
    pos_s = past_len + jnp.arange(dlen, dtype=jnp.int32)
    xp, xs = x_prompt, x_sample
    kp_all, vp_all, cp_all, ks_all, vs_all, cs_all = [], [], [], [], [], []
    for l in range(DEPTH):
        q, k, v, u = in_projection(xp, norm1_g[l], w_in[l], b_in[l])
        q, k = rope(q, pos_p), rope(k, pos_p)
        attn = moba_prompt(q, k, v, pos_p)
        u_hist = jnp.pad(u, ((0, 0), (CONV_LEN - 1, 0), (0, 0)))
        conv = conv_tail(u_hist, conv_w[l], conv_b[l], conv_ln_g[l], conv_ln_b[l])
        xp = post_mixer(xp, attn, conv, w_out[l], norm2_g[l], w_group[l], b_group[l], w_router[l],
                        b_router[l], w_gate[l], w_up[l], w_down[l])
        kp_all.append(k)
        vp_all.append(v)
        cp_all.append(u_hist[:, u_hist.shape[1] - (CONV_LEN - 1):])
        q, k, v, u = in_projection(xs, norm1_g[l], w_in[l], b_in[l])
        q, k = rope(q, pos_s), rope(k, pos_s)
        past_k = cache_k[l, page_table].reshape(dbsz, past_len, N_HEADS, HEAD_DIM)
        past_v = cache_v[l, page_table].reshape(dbsz, past_len, N_HEADS, HEAD_DIM)
        attn = moba_sample(q, k, v, past_k, past_v, pos_s)
        u_hist = jnp.concatenate([state_conv[l], u], axis=1)
        conv = conv_tail(u_hist, conv_w[l], conv_b[l], conv_ln_g[l], conv_ln_b[l])
        xs = post_mixer(xs, attn, conv, w_out[l], norm2_g[l], w_group[l], b_group[l], w_router[l],
                        b_router[l], w_gate[l], w_up[l], w_down[l])
        ks_all.append(k)
        vs_all.append(v)
        cs_all.append(u_hist[:, u_hist.shape[1] - (CONV_LEN - 1):])
    y_prompt = rms_norm(xp, norm_f_g)
    y_sample = rms_norm(xs, norm_f_g)
    return (y_prompt, y_sample, jnp.stack(kp_all), jnp.stack(vp_all), jnp.stack(cp_all),
            jnp.stack(ks_all), jnp.stack(vs_all), jnp.stack(cs_all))
```

```python
import functools
import math

import jax
import jax.numpy as jnp
import numpy as np
from jax import lax
from jax.experimental import pallas as pl
from jax.experimental.pallas import tpu as pltpu

F32 = jnp.float32
BF16 = jnp.bfloat16
HIGHEST = lax.Precision.HIGHEST

D_MODEL = 1024
ATTN_WIDTH = 512
CONV_CH = 512
HEAD_DIM = 64
N_HEADS = 8
IN_WIDTH = 3 * ATTN_WIDTH + 2 * CONV_CH
CONV_LEN = 31
MOBA_BLOCK = 256
MOBA_TOPK = 3
ROPE_THETA = 10000.0
N_EXPERT_GROUPS = 4
EXPERTS_PER_GROUP = 4
N_EXPERTS = 16
D_EXPERT = 512
RMS_EPS = 1e-6
LN_EPS = 1e-5
NEG_INF = -1e30
SM_SCALE = HEAD_DIM ** -0.5
_ROPE_LOG_STEP = -math.log(ROPE_THETA) / (HEAD_DIM // 2)
ROPE_LOG_STEP_HI = float(np.float32(_ROPE_LOG_STEP))
ROPE_LOG_STEP_LO = _ROPE_LOG_STEP - ROPE_LOG_STEP_HI

LANES = 128
HEADS_PER_LANE_TILE = LANES // HEAD_DIM
ROUTER_COLS = LANES
GROUP_COL0 = N_EXPERTS
VMEM_LIMIT = 56 * 1024 * 1024
CONV_HALO = 32
CONV_CHUNK = 64
KMEAN_RING = 8
PAGES_PER_BLOCK = 2


def _cparams(*sem):
    return pltpu.CompilerParams(dimension_semantics=sem, vmem_limit_bytes=VMEM_LIMIT)


def _inproj_kernel(x_ref, g_ref, w_ref, b_ref, q_ref, k_ref, v_ref, u_ref, *prompt_refs,
                   pos0, pos_stride, n_pos_tiles):
    x = x_ref[...]
    tm = x.shape[0]
    h = x * lax.rsqrt(jnp.mean(x * x, axis=-1, keepdims=True) + RMS_EPS) * g_ref[...]
    z = jnp.dot(h.astype(BF16), w_ref[...], preferred_element_type=F32) + b_ref[...]
    a = ATTN_WIDTH
    reps = a // LANES
    half = HEAD_DIM // 2
    lane_t = lax.broadcasted_iota(jnp.int32, (1, LANES), 1)
    freq_ix = (lane_t % half).astype(F32)
    inv_freq = jnp.exp(freq_ix * ROPE_LOG_STEP_HI) * jnp.exp(freq_ix * ROPE_LOG_STEP_LO)
    tile_pos = pos0 + (pl.program_id(0) % n_pos_tiles) * (tm * pos_stride)
    pos = tile_pos + lax.broadcasted_iota(jnp.int32, (tm, 1), 0) * pos_stride
    ang = pos.astype(F32) * inv_freq
    sin_sign = jnp.where((lane_t % HEAD_DIM) < half, -1.0, 1.0)
    cos = jnp.concatenate([jnp.cos(ang)] * reps, axis=-1)
    sin = jnp.concatenate([jnp.sin(ang) * sin_sign] * reps, axis=-1)
    lane = lax.broadcasted_iota(jnp.int32, (1, a), 1)
    first_half = (lane % HEAD_DIM) < half

    def rope(t):
        from_below = pltpu.roll(t, HEAD_DIM // 2, axis=1)
        from_above = pltpu.roll(t, a - HEAD_DIM // 2, axis=1)
        return t * cos + jnp.where(first_half, from_above, from_below) * sin

    q = rope(z[:, 0:a])
    k = rope(z[:, a:2 * a])
    v = z[:, 2 * a:3 * a]
    u = z[:, 3 * a:3 * a + CONV_CH] * jax.nn.sigmoid(z[:, 3 * a + CONV_CH:])
    q_ref[...] = q
    k_ref[...] = k
    v_ref[...] = v
    u_ref[...] = u
    if prompt_refs:
        kb_ref, vb_ref, km_ref = prompt_refs
        kb_ref[...] = k.astype(BF16)
        vb_ref[...] = v.astype(BF16)
        nblk = k.shape[0] // MOBA_BLOCK
        km_ref[0] = jnp.mean(k.reshape(nblk, MOBA_BLOCK, a), axis=1)


def _in_projection(x2d, norm_g, w_bf16, b_in, tm, prompt, pos0, pos_stride, seq_len):
    t = x2d.shape[0]
    n_t = t // tm
    kern = functools.partial(_inproj_kernel, pos0=pos0, pos_stride=pos_stride, n_pos_tiles=seq_len // tm)
    row = lambda i: (i, 0)
    fixed = lambda i: (0, 0)
    wide = jax.ShapeDtypeStruct((t, ATTN_WIDTH), F32)
    out_shape = [wide, wide, wide, wide]
    out_specs = [pl.BlockSpec((tm, ATTN_WIDTH), row)] * 4
    if prompt:
        nblk = tm // MOBA_BLOCK
        out_shape += [jax.ShapeDtypeStruct((t, ATTN_WIDTH), BF16)] * 2
        out_shape += [jax.ShapeDtypeStruct((n_t, nblk, ATTN_WIDTH), F32)]
        out_specs += [pl.BlockSpec((tm, ATTN_WIDTH), row)] * 2
        out_specs += [pl.BlockSpec((1, nblk, ATTN_WIDTH), lambda i: (i, 0, 0))]
    return pl.pallas_call(
        kern,
        grid=(n_t,),
        in_specs=[
            pl.BlockSpec((tm, D_MODEL), row),
            pl.BlockSpec((1, D_MODEL), fixed),
            pl.BlockSpec((D_MODEL, IN_WIDTH), fixed),
            pl.BlockSpec((1, IN_WIDTH), fixed),
        ],
        out_specs=out_specs,
        out_shape=out_shape,
        compiler_params=_cparams("parallel"),
        name="in_projection_prompt" if prompt else "in_projection_sample",
    )(x2d, norm_g, w_bf16, b_in)


def _moba_prompt_kernel(q_ref, kb_ref, vb_ref, km_ref, o_ref, m_sc, l_sc, acc_sc):
    qi = pl.program_id(2)
    blk_rows = MOBA_BLOCK
    q = q_ref[...]
    lane = lax.broadcasted_iota(jnp.int32, (1, LANES), 1)
    head0 = lane < HEAD_DIM
    q2 = jnp.concatenate([jnp.where(head0, q, 0.0), jnp.where(head0, 0.0, q)], axis=0)
    km = km_ref[0]
    n_blocks = km.shape[0]
    gate = lax.dot_general(q2, km, (((1,), (1,)), ((), ())), precision=HIGHEST,
                           preferred_element_type=F32)
    blk = lax.broadcasted_iota(jnp.int32, gate.shape, 1)
    fully_past = blk < qi
    gate = jnp.where(fully_past, gate, NEG_INF)
    rank = jnp.zeros(gate.shape, jnp.int32)
    for m in range(n_blocks):
        gm = gate[:, m:m + 1]
        ahead = (gm > gate) | ((gm == gate) & (m < blk))
        rank = rank + ahead.astype(jnp.int32)
    sel = jnp.where(fully_past & (rank < MOBA_TOPK), 1.0, 0.0)

    qb = (q2 * SM_SCALE).astype(BF16)
    nt = (((1,), (1,)), ((), ()))

    own0 = pl.multiple_of(qi * blk_rows, blk_rows)
    s = lax.dot_general(qb, kb_ref[pl.ds(own0, blk_rows), :], nt, preferred_element_type=F32)
    r_ix = lax.broadcasted_iota(jnp.int32, s.shape, 0) % blk_rows
    c_ix = lax.broadcasted_iota(jnp.int32, s.shape, 1)
    s = jnp.where(c_ix <= r_ix, s, NEG_INF)
    m0 = jnp.max(s, axis=1, keepdims=True)
    p = jnp.exp(s - m0)
    m_sc[...] = m0
    l_sc[...] = jnp.sum(p, axis=1, keepdims=True)
    acc_sc[...] = jnp.dot(p.astype(BF16), vb_ref[pl.ds(own0, blk_rows), :], preferred_element_type=F32)

    def past_block(j, carry):
        r0 = pl.multiple_of(j * blk_rows, blk_rows)
        s = lax.dot_general(qb, kb_ref[pl.ds(r0, blk_rows), :], nt, preferred_element_type=F32)
        chosen = jnp.sum(jnp.where(blk == j, sel, 0.0), axis=1, keepdims=True) > 0.0
        s = jnp.where(chosen, s, NEG_INF)
        m_old = m_sc[...]
        m_new = jnp.maximum(m_old, jnp.max(s, axis=1, keepdims=True))
        alpha = jnp.exp(m_old - m_new)
        p = jnp.exp(s - m_new)
        l_sc[...] = alpha * l_sc[...] + jnp.sum(p, axis=1, keepdims=True)
        acc_sc[...] = alpha * acc_sc[...] + jnp.dot(p.astype(BF16), vb_ref[pl.ds(r0, blk_rows), :],
                                                    preferred_element_type=F32)
        m_sc[...] = m_new
        return carry

    lax.fori_loop(0, qi, past_block, 0)
    out = acc_sc[...] / l_sc[...]
    o_ref[...] = jnp.where(head0, out[:blk_rows], out[blk_rows:]).astype(o_ref.dtype)


def _moba_prompt(q, kb, vb, kmean, bsz, slen):
    n_q = slen // MOBA_BLOCK
    n_hp = ATTN_WIDTH // LANES
    rows2 = HEADS_PER_LANE_TILE * MOBA_BLOCK
    return pl.pallas_call(
        _moba_prompt_kernel,
        grid=(bsz, n_hp, n_q),
        in_specs=[
            pl.BlockSpec((MOBA_BLOCK, LANES), lambda b, hp, qi: (b * n_q + qi, hp)),
            pl.BlockSpec((slen, LANES), lambda b, hp, qi: (b, hp)),
            pl.BlockSpec((slen, LANES), lambda b, hp, qi: (b, hp)),
            pl.BlockSpec((1, n_q, LANES), lambda b, hp, qi: (b, 0, hp)),
        ],
        out_specs=pl.BlockSpec((MOBA_BLOCK, LANES), lambda b, hp, qi: (b * n_q + qi, hp)),
        out_shape=jax.ShapeDtypeStruct((bsz * slen, ATTN_WIDTH), BF16),
        scratch_shapes=[
            pltpu.VMEM((rows2, 1), F32),
            pltpu.VMEM((rows2, 1), F32),
            pltpu.VMEM((rows2, LANES), F32),
        ],
        compiler_params=_cparams("parallel", "parallel", "arbitrary"),
        name="moba_prompt",
    )(q, kb, vb, kmean)


def _ln_swish(c, g, b):
    mu = jnp.mean(c, axis=-1, keepdims=True)
    d = c - mu
    var = jnp.mean(d * d, axis=-1, keepdims=True)
    y = d * lax.rsqrt(var + LN_EPS) * g + b
    return y * jax.nn.sigmoid(y)


def _conv_prompt_kernel(prev_ref, cur_ref, w_ref, cb_ref, g_ref, b_ref, o_ref, ext_sc):
    i = pl.program_id(1)
    tc = cur_ref.shape[0]
    ext_sc[0:CONV_HALO, :] = jnp.where(i > 0, prev_ref[...], 0.0)
    ext_sc[CONV_HALO:CONV_HALO + tc, :] = cur_ref[...]
    lead = CONV_HALO - (CONV_LEN - 1)
    for r0 in range(0, tc, CONV_CHUNK):
        acc = jnp.zeros((CONV_CHUNK, CONV_CH), F32)
        for j in range(CONV_LEN):
            acc = acc + w_ref[j:j + 1, :] * ext_sc[r0 + j + lead:r0 + j + lead + CONV_CHUNK, :]
        y = _ln_swish(acc + cb_ref[...], g_ref[...], b_ref[...])
        o_ref[r0:r0 + CONV_CHUNK, :] = y.astype(o_ref.dtype)


def _conv_prompt(u2d, conv_w, conv_b, ln_g, ln_b, bsz, slen, tc):
    n_c = slen // tc
    halo_per_tile = tc // CONV_HALO
    fixed = lambda b, i: (0, 0)
    return pl.pallas_call(
        _conv_prompt_kernel,
        grid=(bsz, n_c),
        in_specs=[
            pl.BlockSpec((CONV_HALO, CONV_CH),
                         lambda b, i: (jnp.maximum((b * n_c + i) * halo_per_tile - 1, 0), 0)),
            pl.BlockSpec((tc, CONV_CH), lambda b, i: (b * n_c + i, 0)),
            pl.BlockSpec((CONV_LEN, CONV_CH), fixed),
            pl.BlockSpec((1, CONV_CH), fixed),
            pl.BlockSpec((1, CONV_CH), fixed),
            pl.BlockSpec((1, CONV_CH), fixed),
        ],
        out_specs=pl.BlockSpec((tc, CONV_CH), lambda b, i: (b * n_c + i, 0)),
        out_shape=jax.ShapeDtypeStruct((bsz * slen, CONV_CH), BF16),
        scratch_shapes=[pltpu.VMEM((CONV_HALO + tc, CONV_CH), F32)],
        compiler_params=_cparams("parallel", "parallel"),
        name="conv_prompt",
    )(u2d, u2d, conv_w, conv_b, ln_g, ln_b)


def _conv_sample_kernel(st_ref, u_ref, w_ref, cb_ref, g_ref, b_ref, o_ref):
    hist = CONV_LEN - 1
    for b in range(st_ref.shape[0]):
        u_row = u_ref[b:b + 1, :]
        c = jnp.sum(st_ref[b] * w_ref[0:hist, :], axis=0, keepdims=True) + w_ref[hist:hist + 1, :] * u_row
        y = _ln_swish(c + cb_ref[...], g_ref[...], b_ref[...])
        o_ref[b:b + 1, :] = y.astype(o_ref.dtype)


def _conv_sample(state, u2d, conv_w, conv_b, ln_g, ln_b):
    n = u2d.shape[0]
    return pl.pallas_call(
        _conv_sample_kernel,
        out_shape=jax.ShapeDtypeStruct((n, CONV_CH), BF16),
        compiler_params=pltpu.CompilerParams(vmem_limit_bytes=VMEM_LIMIT),
        name="conv_sample",
    )(state, u2d, conv_w, conv_b, ln_g, ln_b)


def _route(logits):
    c = lax.broadcasted_iota(jnp.int32, logits.shape, 1)
    big = jnp.int32(ROUTER_COLS)
    is_group = (c >= GROUP_COL0) & (c < GROUP_COL0 + N_EXPERT_GROUPS)
    gl = jnp.where(is_group, logits, NEG_INF)
    gmax = jnp.max(gl, axis=1, keepdims=True)
    g_sel = jnp.min(jnp.where(gl == gmax, c - GROUP_COL0, big), axis=1, keepdims=True)
    g_w = 1.0 / jnp.sum(jnp.exp(gl - gmax), axis=1, keepdims=True)
    in_group = (c < N_EXPERTS) & ((c // EXPERTS_PER_GROUP) == g_sel)
    el = jnp.where(in_group, logits, NEG_INF)
    m1 = jnp.max(el, axis=1, keepdims=True)
    i1 = jnp.min(jnp.where(el == m1, c, big), axis=1, keepdims=True)
    el2 = jnp.where(c == i1, NEG_INF, el)
    m2 = jnp.max(el2, axis=1, keepdims=True)
    i2 = jnp.min(jnp.where(el2 == m2, c, big), axis=1, keepdims=True)
    e21 = jnp.exp(m2 - m1)
    inv = 1.0 / (1.0 + e21)
    return g_w * jnp.where(c == i1, inv, jnp.where(c == i2, e21 * inv, 0.0))


def _post_kernel(x_ref, a_ref, c_ref, wo_ref, g2_ref, wr_ref, br_ref, wg_ref, wu_ref, wd_ref, gf_ref,
                 y_ref, x1_sc, h2_sc, comb_sc, acc_sc):
    e = pl.program_id(1)

    @pl.when(e == 0)
    def _():
        x1 = (x_ref[...]
              + jnp.dot(a_ref[...], wo_ref[0:ATTN_WIDTH, :], preferred_element_type=F32)
              + jnp.dot(c_ref[...], wo_ref[ATTN_WIDTH:D_MODEL, :], preferred_element_type=F32))
        x1_sc[...] = x1
        h2 = x1 * lax.rsqrt(jnp.mean(x1 * x1, axis=-1, keepdims=True) + RMS_EPS) * g2_ref[...]
        h2_sc[...] = h2.astype(BF16)
        logits = jnp.dot(h2, wr_ref[...], precision=HIGHEST, preferred_element_type=F32) + br_ref[...]
        comb_sc[...] = _route(logits)
        acc_sc[...] = jnp.zeros_like(acc_sc)

    h2b = h2_sc[...]
    col = lax.broadcasted_iota(jnp.int32, comb_sc.shape, 1)
    ce = jnp.sum(jnp.where(col == e, comb_sc[...], 0.0), axis=1, keepdims=True)
    gt = jnp.dot(h2b, wg_ref[0], preferred_element_type=F32)
    up = jnp.dot(h2b, wu_ref[0], preferred_element_type=F32)
    hid = gt * jax.nn.sigmoid(gt) * up * ce
    acc_sc[...] += jnp.dot(hid.astype(BF16), wd_ref[0], preferred_element_type=F32)

    @pl.when(e == N_EXPERTS - 1)
    def _():
        x2 = x1_sc[...] + acc_sc[...]
        y_ref[...] = x2 * lax.rsqrt(jnp.mean(x2 * x2, axis=-1, keepdims=True) + RMS_EPS) * gf_ref[...]


def _post_mixer(x2d, attn, conv, w_out, g2, w_rt, b_rt, wg, wu, wd, gf, tm, name):
    t = x2d.shape[0]
    row = lambda i, e: (i, 0)
    fixed = lambda i, e: (0, 0)
    return pl.pallas_call(
        _post_kernel,
        grid=(t // tm, N_EXPERTS),
        in_specs=[
            pl.BlockSpec((tm, D_MODEL), row),
            pl.BlockSpec((tm, ATTN_WIDTH), row),
            pl.BlockSpec((tm, CONV_CH), row),
            pl.BlockSpec((D_MODEL, D_MODEL), fixed),
            pl.BlockSpec((1, D_MODEL), fixed),
            pl.BlockSpec((D_MODEL, ROUTER_COLS), fixed),
            pl.BlockSpec((1, ROUTER_COLS), fixed),
            pl.BlockSpec((1, D_MODEL, D_EXPERT), lambda i, e: (e, 0, 0)),
            pl.BlockSpec((1, D_MODEL, D_EXPERT), lambda i, e: (e, 0, 0)),
            pl.BlockSpec((1, D_EXPERT, D_MODEL), lambda i, e: (e, 0, 0)),
            pl.BlockSpec((1, D_MODEL), fixed),
        ],
        out_specs=pl.BlockSpec((tm, D_MODEL), row),
        out_shape=jax.ShapeDtypeStruct((t, D_MODEL), F32),
        scratch_shapes=[
            pltpu.VMEM((tm, D_MODEL), F32),
            pltpu.VMEM((tm, D_MODEL), BF16),
            pltpu.VMEM((tm, ROUTER_COLS), F32),
            pltpu.VMEM((tm, D_MODEL), F32),
        ],
        compiler_params=_cparams("parallel", "arbitrary"),
        name=name,
    )(x2d, attn, conv, w_out, g2, w_rt, b_rt, wg, wu, wd, gf)


def _kmean_select_kernel(pt_ref, q_ref, ck_hbm, sel_ref, buf, sem, sums_sc, *, n_pages, page_rows):
    b = pl.program_id(0)
    n_blocks = n_pages // PAGES_PER_BLOCK
    page_flat = page_rows * N_HEADS

    def page_copy(p, slot):
        start = pl.multiple_of(pt_ref[b * n_pages + p] * page_flat, page_flat)
        return pltpu.make_async_copy(ck_hbm.at[pl.ds(start, page_flat), :], buf.at[slot], sem.at[slot])

    for s in range(KMEAN_RING):
        page_copy(s, s).start()

    def block_body(n, carry):
        total = jnp.zeros((N_HEADS, HEAD_DIM), F32)
        for pg in range(PAGES_PER_BLOCK):
            p = n * PAGES_PER_BLOCK + pg
            slot = p % KMEAN_RING
            page_copy(p, slot).wait()
            total = total + jnp.sum(buf[slot].reshape(page_rows, N_HEADS, HEAD_DIM), axis=0)

            @pl.when(p + KMEAN_RING < n_pages)
            def _():
                page_copy(p + KMEAN_RING, slot).start()
        sums_sc[n] = total
        return carry

    lax.fori_loop(0, n_blocks, block_body, 0)

    km = sums_sc[...].reshape(n_blocks * N_HEADS, HEAD_DIM) * (1.0 / MOBA_BLOCK)
    g_all = lax.dot_general(q_ref[0], km, (((1,), (1,)), ((), ())), precision=HIGHEST,
                            preferred_element_type=F32)
    width = n_blocks * N_HEADS
    col = lax.broadcasted_iota(jnp.int32, (N_HEADS, width), 1)
    row = lax.broadcasted_iota(jnp.int32, (N_HEADS, width), 0)
    gate = jnp.sum(jnp.where(col % N_HEADS == row, g_all, 0.0), axis=0, keepdims=True)
    c1 = lax.broadcasted_iota(jnp.int32, (1, width), 1)
    blk = c1 // N_HEADS
    rank = jnp.zeros((1, width), jnp.int32)
    for s in range(1, n_blocks):
        other = pltpu.roll(gate, s * N_HEADS, axis=1)
        other_blk = jnp.where(blk >= s, blk - s, blk - s + n_blocks)
        ahead = (other > gate) | ((other == gate) & (other_blk < blk))
        rank = rank + ahead.astype(jnp.int32)
    blk_f = blk.astype(F32)
    for r in range(MOBA_TOPK):
        picked = jnp.where(rank == r, blk_f, 0.0)
        span = width
        while span > N_HEADS:
            span //= 2
            picked = picked + pltpu.roll(picked, span, axis=1)
        sel_ref[0, r:r + 1, :] = picked[:, 0:LANES].astype(jnp.int32)


def _kmean_select(page_table_flat, q3, cache_k2d, n_seq, n_pages, page_rows):
    n_blocks = n_pages // PAGES_PER_BLOCK
    kern = functools.partial(_kmean_select_kernel, n_pages=n_pages, page_rows=page_rows)
    return pl.pallas_call(
        kern,
        grid_spec=pltpu.PrefetchScalarGridSpec(
            num_scalar_prefetch=1,
            grid=(n_seq,),
            in_specs=[
                pl.BlockSpec((1, N_HEADS, HEAD_DIM), lambda b, pt: (b, 0, 0)),
                pl.BlockSpec(memory_space=pl.ANY),
            ],
            out_specs=pl.BlockSpec((1, MOBA_TOPK, LANES), lambda b, pt: (b, 0, 0)),
            scratch_shapes=[
                pltpu.VMEM((KMEAN_RING, page_rows * N_HEADS, HEAD_DIM), F32),
                pltpu.SemaphoreType.DMA((KMEAN_RING,)),
                pltpu.VMEM((n_blocks, N_HEADS, HEAD_DIM), F32),
            ],
        ),
        out_shape=jax.ShapeDtypeStruct((n_seq, MOBA_TOPK, LANES), jnp.int32),
        compiler_params=_cparams("arbitrary"),
        name="kmean_select_sample",
    )(page_table_flat, q3, cache_k2d)


def _moba_sample_kernel(pt_ref, sel_ref, q_ref, kn_ref, vn_ref, ck_hbm, cv_hbm, o_ref,
                        kbuf, vbuf, ksem, vsem, *, n_pages, page_rows):
    b = pl.program_id(0)
    page_flat = page_rows * N_HEADS
    n_slots = MOBA_TOPK * PAGES_PER_BLOCK

    def head_copies(h, half):
        copies = []
        for r in range(MOBA_TOPK):
            blk = sel_ref[(b * MOBA_TOPK + r) * N_HEADS + h]
            for pg in range(PAGES_PER_BLOCK):
                j = r * PAGES_PER_BLOCK + pg
                start = pl.multiple_of(pt_ref[b * n_pages + blk * PAGES_PER_BLOCK + pg] * page_flat, page_flat)
                src = pl.ds(start, page_flat)
                copies.append(pltpu.make_async_copy(ck_hbm.at[src, :], kbuf.at[half, j], ksem.at[half, j]))
                copies.append(pltpu.make_async_copy(cv_hbm.at[src, :], vbuf.at[half, j], vsem.at[half, j]))
        return copies

    q8 = q_ref[0]
    kn = kn_ref[0]
    vn = vn_ref[0]
    nt = (((1,), (1,)), ((), ()))
    cur = head_copies(0, 0)
    for cp in cur:
        cp.start()
    for h in range(N_HEADS):
        half = h % 2
        nxt = head_copies(h + 1, 1 - half) if h + 1 < N_HEADS else []
        for cp in nxt:
            cp.start()
        for cp in cur:
            cp.wait()
        head_rows = pl.ds(h, page_rows, stride=N_HEADS)
        logits = []
        for j in range(n_slots):
            kh = kbuf[half, j, head_rows, :]
            lj = lax.dot_general(q8, kh, nt, precision=HIGHEST, preferred_element_type=F32)
            logits.append(lj[h:h + 1, :])
        l_past = jnp.concatenate(logits, axis=1) * SM_SCALE
        l_self = jnp.sum(q8[h:h + 1, :] * kn[h:h + 1, :], axis=1, keepdims=True) * SM_SCALE
        m = jnp.maximum(jnp.max(l_past, axis=1, keepdims=True), l_self)
        p = jnp.exp(l_past - m)
        p_self = jnp.exp(l_self - m)
        den = jnp.sum(p, axis=1, keepdims=True) + p_self
        o = p_self * vn[h:h + 1, :]
        for j in range(n_slots):
            vh = vbuf[half, j, head_rows, :]
            o = o + jnp.dot(p[:, j * page_rows:(j + 1) * page_rows], vh, precision=HIGHEST,
                            preferred_element_type=F32)
        o_ref[0, h:h + 1, :] = o / den
        cur = nxt


def _moba_sample(page_table_flat, sel_flat, q3, k3, v3, cache_k2d, cache_v2d, n_seq, n_pages, page_rows):
    kern = functools.partial(_moba_sample_kernel, n_pages=n_pages, page_rows=page_rows)
    n_slots = MOBA_TOPK * PAGES_PER_BLOCK
    tok = pl.BlockSpec((1, N_HEADS, HEAD_DIM), lambda b, pt, sel: (b, 0, 0))
    return pl.pallas_call(
        kern,
        grid_spec=pltpu.PrefetchScalarGridSpec(
            num_scalar_prefetch=2,
            grid=(n_seq,),
            in_specs=[tok, tok, tok, pl.BlockSpec(memory_space=pl.ANY), pl.BlockSpec(memory_space=pl.ANY)],
            out_specs=tok,
            scratch_shapes=[
                pltpu.VMEM((2, n_slots, page_rows * N_HEADS, HEAD_DIM), F32),
                pltpu.VMEM((2, n_slots, page_rows * N_HEADS, HEAD_DIM), F32),
                pltpu.SemaphoreType.DMA((2, n_slots)),
                pltpu.SemaphoreType.DMA((2, n_slots)),
            ],
        ),
        out_shape=jax.ShapeDtypeStruct((n_seq, N_HEADS, HEAD_DIM), F32),
        compiler_params=_cparams("arbitrary"),
        name="moba_sample",
    )(page_table_flat, sel_flat, q3, k3, v3, cache_k2d, cache_v2d)


def _token_tile(t, want):
    return want if t % want == 0 else t


def kernel(x_prompt, x_sample, cache_k, cache_v, state_conv, page_table, norm1_g, w_in, b_in, conv_w, conv_b,
           conv_ln_g, conv_ln_b, w_out, norm2_g, w_group, b_group, w_router, b_router, w_gate, w_up, w_down,
           norm_f_g):
    bsz, slen, _ = x_prompt.shape
    dbsz, dlen, _ = x_sample.shape
    depth = w_in.shape[0]
    assert depth == 1 and dlen == 1, "one layer and one new sample token per sequence"
    n_pages, page_rows = page_table.shape[1], cache_k.shape[2]
    assert MOBA_BLOCK == PAGES_PER_BLOCK * page_rows and slen % MOBA_BLOCK == 0
    assert n_pages // PAGES_PER_BLOCK >= MOBA_TOPK
    past_len = n_pages * page_rows
    l = 0

    w_in_b = w_in[l].astype(BF16)
    w_out_b = w_out[l].astype(BF16)
    wg_b, wu_b, wd_b = w_gate[l].astype(BF16), w_up[l].astype(BF16), w_down[l].astype(BF16)
    pad_cols = ROUTER_COLS - N_EXPERTS - N_EXPERT_GROUPS
    w_rt = jnp.pad(jnp.concatenate([w_router[l], w_group[l]], axis=1), ((0, 0), (0, pad_cols)))
    b_rt = jnp.pad(jnp.concatenate([b_router[l], b_group[l]])[None, :], ((0, 0), (0, pad_cols)))
    g1, g2, gf = norm1_g[l][None, :], norm2_g[l][None, :], norm_f_g[None, :]
    b_in2, cb = b_in[l][None, :], conv_b[l][None, :]
    lg, lb = conv_ln_g[l][None, :], conv_ln_b[l][None, :]

    t_p = bsz * slen
    xp = x_prompt.reshape(t_p, D_MODEL)
    tm_in = _token_tile(slen, 512)
    q_p, k_p, v_p, u_p, kb_p, vb_p, km_p = _in_projection(xp, g1, w_in_b, b_in2, tm_in, True, 0, 1, slen)
    km_p = km_p.reshape(bsz, slen // MOBA_BLOCK, ATTN_WIDTH)
    attn_p = _moba_prompt(q_p, kb_p, vb_p, km_p, bsz, slen)
    conv_p = _conv_prompt(u_p, conv_w[l], cb, lg, lb, bsz, slen, _token_tile(slen, 256))
    y_p = _post_mixer(xp, attn_p, conv_p, w_out_b, g2, w_rt, b_rt, wg_b, wu_b, wd_b, gf,
                      _token_tile(t_p, 512), "post_mixer_prompt")

    xs = x_sample.reshape(dbsz, D_MODEL)
    q_s, k_s, v_s, u_s = _in_projection(xs, g1, w_in_b, b_in2, dbsz, False, past_len, 0, dbsz)
    heads = lambda t: t.reshape(dbsz, N_HEADS, HEAD_DIM)
    ck2 = cache_k[l].reshape(-1, HEAD_DIM)
    cv2 = cache_v[l].reshape(-1, HEAD_DIM)
    pt_flat = page_table.reshape(-1)
    sel = _kmean_select(pt_flat, heads(q_s), ck2, dbsz, n_pages, page_rows)
    sel_flat = sel[:, :, :N_HEADS].reshape(-1)
    attn_s = _moba_sample(pt_flat, sel_flat, heads(q_s), heads(k_s), heads(v_s), ck2, cv2,
                          dbsz, n_pages, page_rows)
    conv_s = _conv_sample(state_conv[l], u_s, conv_w[l], cb, lg, lb)
    y_s = _post_mixer(xs, attn_s.reshape(dbsz, ATTN_WIDTH).astype(BF16), conv_s, w_out_b, g2, w_rt, b_rt,
                      wg_b, wu_b, wd_b, gf, dbsz, "post_mixer_sample")

    hist = CONV_LEN - 1
    kv_p = lambda t: t.reshape(1, bsz, slen, N_HEADS, HEAD_DIM)
    kv_s = lambda t: t.reshape(1, dbsz, 1, N_HEADS, HEAD_DIM)
    new_conv_p = u_p.reshape(bsz, slen, CONV_CH)[:, slen - hist:][None]
    new_conv_s = jnp.concatenate([state_conv[l][:, 1:], u_s[:, None, :]], axis=1)[None]
    return (y_p.reshape(bsz, slen, D_MODEL), y_s.reshape(dbsz, 1, D_MODEL), kv_p(k_p), kv_p(v_p), new_conv_p,
            kv_s(k_s), kv_s(v_s), new_conv_s)
```

```python
import functools
import math

import jax
import jax.numpy as jnp
import numpy as np
from jax import lax
from jax.experimental import pallas as pl
from jax.experimental.pallas import tpu as pltpu

F32 = jnp.float32
BF16 = jnp.bfloat16
HIGHEST = lax.Precision.HIGHEST

D_MODEL = 1024
ATTN_WIDTH = 512
CONV_CH = 512
HEAD_DIM = 64
N_HEADS = 8
IN_WIDTH = 3 * ATTN_WIDTH + 2 * CONV_CH
CONV_LEN = 31
MOBA_BLOCK = 256
MOBA_TOPK = 3
ROPE_THETA = 10000.0
N_EXPERT_GROUPS = 4
EXPERTS_PER_GROUP = 4
N_EXPERTS = 16
D_EXPERT = 512
RMS_EPS = 1e-6
LN_EPS = 1e-5
NEG_INF = -1e30
SM_SCALE = HEAD_DIM ** -0.5
_ROPE_LOG_STEP = -math.log(ROPE_THETA) / (HEAD_DIM // 2)
ROPE_LOG_STEP_HI = float(np.float32(_ROPE_LOG_STEP))
ROPE_LOG_STEP_LO = _ROPE_LOG_STEP - ROPE_LOG_STEP_HI

LANES = 128
HEADS_PER_LANE_TILE = LANES // HEAD_DIM
ROUTER_COLS = LANES
GROUP_COL0 = N_EXPERTS
VMEM_LIMIT = 56 * 1024 * 1024
CONV_HALO = 32
CONV_CHUNK = 64
KMEAN_RING = 8
PAGES_PER_BLOCK = 2


def _cparams(*sem):
    return pltpu.CompilerParams(dimension_semantics=sem, vmem_limit_bytes=VMEM_LIMIT)


def _inproj_kernel(x_ref, g_ref, w_ref, b_ref, q_ref, k_ref, v_ref, u_ref, *prompt_refs,
                   pos0, pos_stride, n_pos_tiles):
    x = x_ref[...]
    tm = x.shape[0]
    h = x * lax.rsqrt(jnp.mean(x * x, axis=-1, keepdims=True) + RMS_EPS) * g_ref[...]
    z = jnp.dot(h.astype(BF16), w_ref[...], preferred_element_type=F32) + b_ref[...]
    a = ATTN_WIDTH
    reps = a // LANES
    half = HEAD_DIM // 2
    lane_t = lax.broadcasted_iota(jnp.int32, (1, LANES), 1)
    freq_ix = (lane_t % half).astype(F32)
    inv_freq = jnp.exp(freq_ix * ROPE_LOG_STEP_HI) * jnp.exp(freq_ix * ROPE_LOG_STEP_LO)
    tile_pos = pos0 + (pl.program_id(0) % n_pos_tiles) * (tm * pos_stride)
    pos = tile_pos + lax.broadcasted_iota(jnp.int32, (tm, 1), 0) * pos_stride
    ang = pos.astype(F32) * inv_freq
    sin_sign = jnp.where((lane_t % HEAD_DIM) < half, -1.0, 1.0)
    cos = jnp.concatenate([jnp.cos(ang)] * reps, axis=-1)
    sin = jnp.concatenate([jnp.sin(ang) * sin_sign] * reps, axis=-1)
    lane = lax.broadcasted_iota(jnp.int32, (1, a), 1)
    first_half = (lane % HEAD_DIM) < half

    def rope(t):
        from_below = pltpu.roll(t, HEAD_DIM // 2, axis=1)
        from_above = pltpu.roll(t, a - HEAD_DIM // 2, axis=1)
        return t * cos + jnp.where(first_half, from_above, from_below) * sin

    q = rope(z[:, 0:a])
    k = rope(z[:, a:2 * a])
    v = z[:, 2 * a:3 * a]
    u = z[:, 3 * a:3 * a + CONV_CH] * jax.nn.sigmoid(z[:, 3 * a + CONV_CH:])
    q_ref[...] = q
    u_ref[...] = u
    if prompt_refs:
        kb_ref, vtb_ref, km_ref = prompt_refs
        v_t = v.T
        k_ref[...] = k.T
        v_ref[...] = v_t
        kb_ref[...] = k.astype(BF16)
        vtb_ref[...] = v_t.astype(BF16)
        nblk = k.shape[0] // MOBA_BLOCK
        km_ref[0] = jnp.mean(k.reshape(nblk, MOBA_BLOCK, a), axis=1)
    else:
        k_ref[...] = k
        v_ref[...] = v


def _in_projection(x2d, norm_g, w_bf16, b_in, tm, prompt, pos0, pos_stride, seq_len):
    t = x2d.shape[0]
    n_t = t // tm
    kern = functools.partial(_inproj_kernel, pos0=pos0, pos_stride=pos_stride, n_pos_tiles=seq_len // tm)
    row = lambda i: (i, 0)
    fixed = lambda i: (0, 0)
    n_pos = seq_len // tm
    wide = jax.ShapeDtypeStruct((t, ATTN_WIDTH), F32)
    wide_spec = pl.BlockSpec((tm, ATTN_WIDTH), row)
    if prompt:
        nblk = tm // MOBA_BLOCK
        n_seq = t // seq_len
        feat_spec = pl.BlockSpec((ATTN_WIDTH, tm), lambda i: (i // n_pos, i % n_pos))
        feat_f32 = jax.ShapeDtypeStruct((n_seq * ATTN_WIDTH, seq_len), F32)
        feat_bf16 = jax.ShapeDtypeStruct((n_seq * ATTN_WIDTH, seq_len), BF16)
        out_shape = [wide, feat_f32, feat_f32, wide, jax.ShapeDtypeStruct((t, ATTN_WIDTH), BF16), feat_bf16,
                     jax.ShapeDtypeStruct((n_t, nblk, ATTN_WIDTH), F32)]
        out_specs = [wide_spec, feat_spec, feat_spec, wide_spec, wide_spec, feat_spec,
                     pl.BlockSpec((1, nblk, ATTN_WIDTH), lambda i: (i, 0, 0))]
    else:
        out_shape = [wide] * 4
        out_specs = [wide_spec] * 4
    return pl.pallas_call(
        kern,
        grid=(n_t,),
        in_specs=[
            pl.BlockSpec((tm, D_MODEL), row),
            pl.BlockSpec((1, D_MODEL), fixed),
            pl.BlockSpec((D_MODEL, IN_WIDTH), fixed),
            pl.BlockSpec((1, IN_WIDTH), fixed),
        ],
        out_specs=out_specs,
        out_shape=out_shape,
        compiler_params=_cparams("parallel"),
        name="in_projection_prompt" if prompt else "in_projection_sample",
    )(x2d, norm_g, w_bf16, b_in)


def _moba_prompt_kernel(q_ref, kb_ref, vt_ref, km_ref, o_ref, m_sc, l_sc, acc_sc, bias_sc):
    qi = pl.program_id(2)
    blk_rows = MOBA_BLOCK
    q = q_ref[...]
    lane = lax.broadcasted_iota(jnp.int32, (1, LANES), 1)
    head0 = lane < HEAD_DIM
    q2 = jnp.concatenate([jnp.where(head0, q, 0.0), jnp.where(head0, 0.0, q)], axis=0)
    km = km_ref[0]
    n_blocks = km.shape[0]
    nt = (((1,), (1,)), ((), ()))
    gate = lax.dot_general(km, q2, nt, precision=HIGHEST, preferred_element_type=F32)
    blk = lax.broadcasted_iota(jnp.int32, gate.shape, 0)
    fully_past = blk < qi
    gate = jnp.where(fully_past, gate, NEG_INF)
    rank = jnp.zeros(gate.shape, jnp.int32)
    for m in range(n_blocks):
        gm = gate[m:m + 1, :]
        ahead = (gm > gate) | ((gm == gate) & (m < blk))
        rank = rank + ahead.astype(jnp.int32)
    bias_sc[...] = jnp.where(fully_past & (rank < MOBA_TOPK), 0.0, NEG_INF)

    qb = (q2 * SM_SCALE).astype(BF16)

    own0 = pl.multiple_of(qi * blk_rows, blk_rows)
    s = lax.dot_general(kb_ref[pl.ds(own0, blk_rows), :], qb, nt, preferred_element_type=F32)
    k_ix = lax.broadcasted_iota(jnp.int32, s.shape, 0)
    q_ix = lax.broadcasted_iota(jnp.int32, s.shape, 1) % blk_rows
    s = jnp.where(k_ix <= q_ix, s, NEG_INF)
    m0 = jnp.max(s, axis=0, keepdims=True)
    p = jnp.exp(s - m0)
    m_sc[...] = m0
    l_sc[...] = jnp.sum(p, axis=0, keepdims=True)
    acc_sc[...] = jnp.dot(vt_ref[:, pl.ds(own0, blk_rows)], p.astype(BF16), preferred_element_type=F32)

    def past_block(j, carry):
        r0 = pl.multiple_of(j * blk_rows, blk_rows)
        s = lax.dot_general(kb_ref[pl.ds(r0, blk_rows), :], qb, nt, preferred_element_type=F32)
        s = s + bias_sc[pl.ds(j, 1), :]
        m_old = m_sc[...]
        m_new = jnp.maximum(m_old, jnp.max(s, axis=0, keepdims=True))
        alpha = jnp.exp(m_old - m_new)
        p = jnp.exp(s - m_new)
        l_sc[...] = alpha * l_sc[...] + jnp.sum(p, axis=0, keepdims=True)
        acc_sc[...] = alpha * acc_sc[...] + jnp.dot(vt_ref[:, pl.ds(r0, blk_rows)], p.astype(BF16),
                                                    preferred_element_type=F32)
        m_sc[...] = m_new
        return carry

    lax.fori_loop(0, qi, past_block, 0)
    out = acc_sc[...] / l_sc[...]
    out_t = jnp.concatenate([out[:HEAD_DIM, :blk_rows], out[HEAD_DIM:, blk_rows:]], axis=0)
    o_ref[...] = out_t.T.astype(o_ref.dtype)


def _moba_prompt(q, kb, vtb, kmean, bsz, slen):
    n_q = slen // MOBA_BLOCK
    n_hp = ATTN_WIDTH // LANES
    cols2 = HEADS_PER_LANE_TILE * MOBA_BLOCK
    return pl.pallas_call(
        _moba_prompt_kernel,
        grid=(bsz, n_hp, n_q),
        in_specs=[
            pl.BlockSpec((MOBA_BLOCK, LANES), lambda b, hp, qi: (b * n_q + qi, hp)),
            pl.BlockSpec((slen, LANES), lambda b, hp, qi: (b, hp)),
            pl.BlockSpec((LANES, slen), lambda b, hp, qi: (b * n_hp + hp, 0)),
            pl.BlockSpec((1, n_q, LANES), lambda b, hp, qi: (b, 0, hp)),
        ],
        out_specs=pl.BlockSpec((MOBA_BLOCK, LANES), lambda b, hp, qi: (b * n_q + qi, hp)),
        out_shape=jax.ShapeDtypeStruct((bsz * slen, ATTN_WIDTH), BF16),
        scratch_shapes=[
            pltpu.VMEM((1, cols2), F32),
            pltpu.VMEM((1, cols2), F32),
            pltpu.VMEM((LANES, cols2), F32),
            pltpu.VMEM((n_q, cols2), F32),
        ],
        compiler_params=_cparams("parallel", "parallel", "arbitrary"),
        name="moba_prompt",
    )(q, kb, vtb, kmean)


def _ln_swish(c, g, b):
    mu = jnp.mean(c, axis=-1, keepdims=True)
    d = c - mu
    var = jnp.mean(d * d, axis=-1, keepdims=True)
    y = d * lax.rsqrt(var + LN_EPS) * g + b
    return y * jax.nn.sigmoid(y)


def _conv_prompt_kernel(prev_ref, cur_ref, w_ref, cb_ref, g_ref, b_ref, o_ref, ext_sc):
    i = pl.program_id(1)
    tc = cur_ref.shape[0]
    ext_sc[0:CONV_HALO, :] = jnp.where(i > 0, prev_ref[...], 0.0)
    ext_sc[CONV_HALO:CONV_HALO + tc, :] = cur_ref[...]
    lead = CONV_HALO - (CONV_LEN - 1)
    for r0 in range(0, tc, CONV_CHUNK):
        acc = jnp.zeros((CONV_CHUNK, CONV_CH), F32)
        for j in range(CONV_LEN):
            acc = acc + w_ref[j:j + 1, :] * ext_sc[r0 + j + lead:r0 + j + lead + CONV_CHUNK, :]
        y = _ln_swish(acc + cb_ref[...], g_ref[...], b_ref[...])
        o_ref[r0:r0 + CONV_CHUNK, :] = y.astype(o_ref.dtype)


def _conv_prompt(u2d, conv_w, conv_b, ln_g, ln_b, bsz, slen, tc):
    n_c = slen // tc
    halo_per_tile = tc // CONV_HALO
    fixed = lambda b, i: (0, 0)
    return pl.pallas_call(
        _conv_prompt_kernel,
        grid=(bsz, n_c),
        in_specs=[
            pl.BlockSpec((CONV_HALO, CONV_CH),
                         lambda b, i: (jnp.maximum((b * n_c + i) * halo_per_tile - 1, 0), 0)),
            pl.BlockSpec((tc, CONV_CH), lambda b, i: (b * n_c + i, 0)),
            pl.BlockSpec((CONV_LEN, CONV_CH), fixed),
            pl.BlockSpec((1, CONV_CH), fixed),
            pl.BlockSpec((1, CONV_CH), fixed),
            pl.BlockSpec((1, CONV_CH), fixed),
        ],
        out_specs=pl.BlockSpec((tc, CONV_CH), lambda b, i: (b * n_c + i, 0)),
        out_shape=jax.ShapeDtypeStruct((bsz * slen, CONV_CH), BF16),
        scratch_shapes=[pltpu.VMEM((CONV_HALO + tc, CONV_CH), F32)],
        compiler_params=_cparams("parallel", "parallel"),
        name="conv_prompt",
    )(u2d, u2d, conv_w, conv_b, ln_g, ln_b)


def _conv_sample_kernel(st_ref, u_ref, w_ref, cb_ref, g_ref, b_ref, o_ref):
    hist = CONV_LEN - 1
    for b in range(st_ref.shape[0]):
        u_row = u_ref[b:b + 1, :]
        c = jnp.sum(st_ref[b] * w_ref[0:hist, :], axis=0, keepdims=True) + w_ref[hist:hist + 1, :] * u_row
        y = _ln_swish(c + cb_ref[...], g_ref[...], b_ref[...])
        o_ref[b:b + 1, :] = y.astype(o_ref.dtype)


def _conv_sample(state, u2d, conv_w, conv_b, ln_g, ln_b):
    n = u2d.shape[0]
    return pl.pallas_call(
        _conv_sample_kernel,
        out_shape=jax.ShapeDtypeStruct((n, CONV_CH), BF16),
        compiler_params=pltpu.CompilerParams(vmem_limit_bytes=VMEM_LIMIT),
        name="conv_sample",
    )(state, u2d, conv_w, conv_b, ln_g, ln_b)


def _route(logits):
    c = lax.broadcasted_iota(jnp.int32, logits.shape, 1)
    big = jnp.int32(ROUTER_COLS)
    is_group = (c >= GROUP_COL0) & (c < GROUP_COL0 + N_EXPERT_GROUPS)
    gl = jnp.where(is_group, logits, NEG_INF)
    gmax = jnp.max(gl, axis=1, keepdims=True)
    g_sel = jnp.min(jnp.where(gl == gmax, c - GROUP_COL0, big), axis=1, keepdims=True)
    g_w = 1.0 / jnp.sum(jnp.exp(gl - gmax), axis=1, keepdims=True)
    in_group = (c < N_EXPERTS) & ((c // EXPERTS_PER_GROUP) == g_sel)
    el = jnp.where(in_group, logits, NEG_INF)
    m1 = jnp.max(el, axis=1, keepdims=True)
    i1 = jnp.min(jnp.where(el == m1, c, big), axis=1, keepdims=True)
    el2 = jnp.where(c == i1, NEG_INF, el)
    m2 = jnp.max(el2, axis=1, keepdims=True)
    i2 = jnp.min(jnp.where(el2 == m2, c, big), axis=1, keepdims=True)
    e21 = jnp.exp(m2 - m1)
    inv = 1.0 / (1.0 + e21)
    return g_w * jnp.where(c == i1, inv, jnp.where(c == i2, e21 * inv, 0.0))


def _post_kernel(x_ref, a_ref, c_ref, wo_ref, g2_ref, wr_ref, br_ref, wg_ref, wu_ref, wd_ref, gf_ref,
                 y_ref, x1_sc, h2_sc, comb_sc, acc_sc):
    e = pl.program_id(1)

    @pl.when(e == 0)
    def _():
        x1 = (x_ref[...]
              + jnp.dot(a_ref[...], wo_ref[0:ATTN_WIDTH, :], preferred_element_type=F32)
              + jnp.dot(c_ref[...], wo_ref[ATTN_WIDTH:D_MODEL, :], preferred_element_type=F32))
        x1_sc[...] = x1
        h2 = x1 * lax.rsqrt(jnp.mean(x1 * x1, axis=-1, keepdims=True) + RMS_EPS) * g2_ref[...]
        h2_sc[...] = h2.astype(BF16)
        logits = jnp.dot(h2, wr_ref[...], precision=HIGHEST, preferred_element_type=F32) + br_ref[...]
        comb_sc[...] = _route(logits)
        acc_sc[...] = jnp.zeros_like(acc_sc)

    h2b = h2_sc[...]
    col = lax.broadcasted_iota(jnp.int32, comb_sc.shape, 1)
    ce = jnp.sum(jnp.where(col == e, comb_sc[...], 0.0), axis=1, keepdims=True)
    gt = jnp.dot(h2b, wg_ref[0], preferred_element_type=F32)
    up = jnp.dot(h2b, wu_ref[0], preferred_element_type=F32)
    hid = gt * jax.nn.sigmoid(gt) * up * ce
    acc_sc[...] += jnp.dot(hid.astype(BF16), wd_ref[0], preferred_element_type=F32)

    @pl.when(e == N_EXPERTS - 1)
    def _():
        x2 = x1_sc[...] + acc_sc[...]
        y_ref[...] = x2 * lax.rsqrt(jnp.mean(x2 * x2, axis=-1, keepdims=True) + RMS_EPS) * gf_ref[...]


def _post_mixer(x2d, attn, conv, w_out, g2, w_rt, b_rt, wg, wu, wd, gf, tm, name):
    t = x2d.shape[0]
    row = lambda i, e: (i, 0)
    fixed = lambda i, e: (0, 0)
    return pl.pallas_call(
        _post_kernel,
        grid=(t // tm, N_EXPERTS),
        in_specs=[
            pl.BlockSpec((tm, D_MODEL), row),
            pl.BlockSpec((tm, ATTN_WIDTH), row),
            pl.BlockSpec((tm, CONV_CH), row),
            pl.BlockSpec((D_MODEL, D_MODEL), fixed),
            pl.BlockSpec((1, D_MODEL), fixed),
            pl.BlockSpec((D_MODEL, ROUTER_COLS), fixed),
            pl.BlockSpec((1, ROUTER_COLS), fixed),
            pl.BlockSpec((1, D_MODEL, D_EXPERT), lambda i, e: (e, 0, 0)),
            pl.BlockSpec((1, D_MODEL, D_EXPERT), lambda i, e: (e, 0, 0)),
            pl.BlockSpec((1, D_EXPERT, D_MODEL), lambda i, e: (e, 0, 0)),
            pl.BlockSpec((1, D_MODEL), fixed),
        ],
        out_specs=pl.BlockSpec((tm, D_MODEL), row),
        out_shape=jax.ShapeDtypeStruct((t, D_MODEL), F32),
        scratch_shapes=[
            pltpu.VMEM((tm, D_MODEL), F32),
            pltpu.VMEM((tm, D_MODEL), BF16),
            pltpu.VMEM((tm, ROUTER_COLS), F32),
            pltpu.VMEM((tm, D_MODEL), F32),
        ],
        compiler_params=_cparams("parallel", "arbitrary"),
        name=name,
    )(x2d, attn, conv, w_out, g2, w_rt, b_rt, wg, wu, wd, gf)


def _kmean_select_kernel(pt_ref, q_ref, ck_hbm, sel_ref, buf, sem, sums_sc, *, n_pages):
    b = pl.program_id(0)
    n_blocks = n_pages // PAGES_PER_BLOCK
    page_flat = ATTN_WIDTH

    def page_copy(p, slot):
        start = pl.multiple_of(pt_ref[b * n_pages + p] * page_flat, page_flat)
        return pltpu.make_async_copy(ck_hbm.at[pl.ds(start, page_flat), :], buf.at[slot], sem.at[slot])

    for s in range(KMEAN_RING):
        page_copy(s, s).start()
    sums_sc[...] = jnp.zeros_like(sums_sc)
    lane = lax.broadcasted_iota(jnp.int32, (1, LANES), 1)

    def block_body(n, carry):
        total = jnp.zeros((page_flat, LANES), F32)
        for pg in range(PAGES_PER_BLOCK):
            p = n * PAGES_PER_BLOCK + pg
            slot = p % KMEAN_RING
            page_copy(p, slot).wait()
            total = total + buf[slot]

            @pl.when(p + KMEAN_RING < n_pages)
            def _():
                page_copy(p + KMEAN_RING, slot).start()
        block_sum = jnp.sum(total, axis=1, keepdims=True)
        sums_sc[...] = jnp.where(lane == n, block_sum, sums_sc[...])
        return carry

    lax.fori_loop(0, n_blocks, block_body, 0)

    km = sums_sc[...] * (1.0 / MOBA_BLOCK)
    q8 = q_ref[0]
    gate = jnp.concatenate(
        [jnp.dot(q8[h:h + 1, :], km[h * HEAD_DIM:(h + 1) * HEAD_DIM, :], precision=HIGHEST,
                 preferred_element_type=F32) for h in range(N_HEADS)], axis=0)
    blk = lax.broadcasted_iota(jnp.int32, gate.shape, 1)
    gate = jnp.where(blk < n_blocks, gate, NEG_INF)
    rank = jnp.zeros(gate.shape, jnp.int32)
    for s in range(1, LANES):
        other = pltpu.roll(gate, s, axis=1)
        other_blk = jnp.where(blk >= s, blk - s, blk - s + LANES)
        ahead = (other > gate) | ((other == gate) & (other_blk < blk))
        rank = rank + ahead.astype(jnp.int32)
    out = jnp.zeros(gate.shape, jnp.int32)
    for r in range(MOBA_TOPK):
        picked = jnp.sum(jnp.where(rank == r, blk, 0).astype(F32), axis=1, keepdims=True)
        out = jnp.where(blk == r, picked.astype(jnp.int32), out)
    sel_ref[0] = out


def _kmean_select(page_table_flat, q3, cache_kt, n_seq, n_pages):
    kern = functools.partial(_kmean_select_kernel, n_pages=n_pages)
    return pl.pallas_call(
        kern,
        grid_spec=pltpu.PrefetchScalarGridSpec(
            num_scalar_prefetch=1,
            grid=(n_seq,),
            in_specs=[
                pl.BlockSpec((1, N_HEADS, HEAD_DIM), lambda b, pt: (b, 0, 0)),
                pl.BlockSpec(memory_space=pl.ANY),
            ],
            out_specs=pl.BlockSpec((1, N_HEADS, LANES), lambda b, pt: (b, 0, 0)),
            scratch_shapes=[
                pltpu.VMEM((KMEAN_RING, ATTN_WIDTH, LANES), F32),
                pltpu.SemaphoreType.DMA((KMEAN_RING,)),
                pltpu.VMEM((ATTN_WIDTH, LANES), F32),
            ],
        ),
        out_shape=jax.ShapeDtypeStruct((n_seq, N_HEADS, LANES), jnp.int32),
        compiler_params=_cparams("arbitrary"),
        name="kmean_select_sample",
    )(page_table_flat, q3, cache_kt)


def _moba_sample_kernel(pt_ref, sel_ref, q_ref, kn_ref, vn_ref, ck_hbm, cv_hbm, o_ref,
                        kbuf, vbuf, ksem, vsem, *, n_pages, n_seq):
    b = pl.program_id(0)
    half = b % 2

    def seq_copies(seq, dst_half):
        copies = []
        for h in range(N_HEADS):
            for r in range(MOBA_TOPK):
                blk = sel_ref[(seq * N_HEADS + h) * MOBA_TOPK + r]
                for pg in range(PAGES_PER_BLOCK):
                    page = pt_ref[seq * n_pages + blk * PAGES_PER_BLOCK + pg]
                    src = pl.ds(pl.multiple_of(page * ATTN_WIDTH + h * HEAD_DIM, HEAD_DIM), HEAD_DIM)
                    dst = pl.ds((r * PAGES_PER_BLOCK + pg) * LANES, LANES)
                    copies.append(pltpu.make_async_copy(ck_hbm.at[src, :], kbuf.at[dst_half, h, :, dst],
                                                        ksem.at[dst_half]))
                    copies.append(pltpu.make_async_copy(cv_hbm.at[src, :], vbuf.at[dst_half, h, :, dst],
                                                        vsem.at[dst_half]))
        return copies

    @pl.when(b == 0)
    def _():
        for cp in seq_copies(0, 0):
            cp.start()

    @pl.when(b + 1 < n_seq)
    def _():
        for cp in seq_copies(b + 1, 1 - half):
            cp.start()

    for cp in seq_copies(b, half):
        cp.wait()

    q8 = q_ref[0]
    kn = kn_ref[0]
    vn = vn_ref[0]
    nt = (((1,), (1,)), ((), ()))
    for h in range(N_HEADS):
        qh = q8[h:h + 1, :]
        l_past = jnp.dot(qh, kbuf[half, h], precision=HIGHEST, preferred_element_type=F32) * SM_SCALE
        l_self = jnp.sum(qh * kn[h:h + 1, :], axis=1, keepdims=True) * SM_SCALE
        m = jnp.maximum(jnp.max(l_past, axis=1, keepdims=True), l_self)
        p = jnp.exp(l_past - m)
        p_self = jnp.exp(l_self - m)
        den = jnp.sum(p, axis=1, keepdims=True) + p_self
        o = p_self * vn[h:h + 1, :] + lax.dot_general(p, vbuf[half, h], nt, precision=HIGHEST,
                                                      preferred_element_type=F32)
        o_ref[0, h:h + 1, :] = o / den


def _moba_sample(page_table_flat, sel_flat, q3, k3, v3, cache_kt, cache_vt, n_seq, n_pages):
    kern = functools.partial(_moba_sample_kernel, n_pages=n_pages, n_seq=n_seq)
    n_keys = MOBA_TOPK * MOBA_BLOCK
    tok = pl.BlockSpec((1, N_HEADS, HEAD_DIM), lambda b, pt, sel: (b, 0, 0))
    return pl.pallas_call(
        kern,
        grid_spec=pltpu.PrefetchScalarGridSpec(
            num_scalar_prefetch=2,
            grid=(n_seq,),
            in_specs=[tok, tok, tok, pl.BlockSpec(memory_space=pl.ANY), pl.BlockSpec(memory_space=pl.ANY)],
            out_specs=tok,
            scratch_shapes=[
                pltpu.VMEM((2, N_HEADS, HEAD_DIM, n_keys), F32),
                pltpu.VMEM((2, N_HEADS, HEAD_DIM, n_keys), F32),
                pltpu.SemaphoreType.DMA((2,)),
                pltpu.SemaphoreType.DMA((2,)),
            ],
        ),
        out_shape=jax.ShapeDtypeStruct((n_seq, N_HEADS, HEAD_DIM), F32),
        compiler_params=_cparams("arbitrary"),
        name="moba_sample",
    )(page_table_flat, sel_flat, q3, k3, v3, cache_kt, cache_vt)


def _token_tile(t, want):
    return want if t % want == 0 else t


def kernel(x_prompt, x_sample, cache_k, cache_v, state_conv, page_table, norm1_g, w_in, b_in, conv_w, conv_b,
           conv_ln_g, conv_ln_b, w_out, norm2_g, w_group, b_group, w_router, b_router, w_gate, w_up, w_down,
           norm_f_g):
    bsz, slen, _ = x_prompt.shape
    dbsz, dlen, _ = x_sample.shape
    depth = w_in.shape[0]
    assert depth == 1 and dlen == 1, "one layer and one new sample token per sequence"
    n_pages, page_rows = page_table.shape[1], cache_k.shape[2]
    assert MOBA_BLOCK == PAGES_PER_BLOCK * page_rows and slen % MOBA_BLOCK == 0
    assert page_rows == LANES and MOBA_TOPK <= n_pages // PAGES_PER_BLOCK <= LANES and n_pages >= KMEAN_RING
    past_len = n_pages * page_rows
    l = 0

    w_in_b = w_in[l].astype(BF16)
    w_out_b = w_out[l].astype(BF16)
    wg_b, wu_b, wd_b = w_gate[l].astype(BF16), w_up[l].astype(BF16), w_down[l].astype(BF16)
    pad_cols = ROUTER_COLS - N_EXPERTS - N_EXPERT_GROUPS
    w_rt = jnp.pad(jnp.concatenate([w_router[l], w_group[l]], axis=1), ((0, 0), (0, pad_cols)))
    b_rt = jnp.pad(jnp.concatenate([b_router[l], b_group[l]])[None, :], ((0, 0), (0, pad_cols)))
    g1, g2, gf = norm1_g[l][None, :], norm2_g[l][None, :], norm_f_g[None, :]
    b_in2, cb = b_in[l][None, :], conv_b[l][None, :]
    lg, lb = conv_ln_g[l][None, :], conv_ln_b[l][None, :]

    t_p = bsz * slen
    xp = x_prompt.reshape(t_p, D_MODEL)
    tm_in = _token_tile(slen, 512)
    q_p, kt_p, vt_p, u_p, kb_p, vtb_p, km_p = _in_projection(xp, g1, w_in_b, b_in2, tm_in, True, 0, 1, slen)
    km_p = km_p.reshape(bsz, slen // MOBA_BLOCK, ATTN_WIDTH)
    attn_p = _moba_prompt(q_p, kb_p, vtb_p, km_p, bsz, slen)
    conv_p = _conv_prompt(u_p, conv_w[l], cb, lg, lb, bsz, slen, _token_tile(slen, 256))
    y_p = _post_mixer(xp, attn_p, conv_p, w_out_b, g2, w_rt, b_rt, wg_b, wu_b, wd_b, gf,
                      _token_tile(t_p, 512), "post_mixer_prompt")

    xs = x_sample.reshape(dbsz, D_MODEL)
    q_s, k_s, v_s, u_s = _in_projection(xs, g1, w_in_b, b_in2, dbsz, False, past_len, 0, dbsz)
    heads = lambda t: t.reshape(dbsz, N_HEADS, HEAD_DIM)
    ck2 = cache_k[l].transpose(0, 2, 3, 1).reshape(-1, page_rows)
    cv2 = cache_v[l].transpose(0, 2, 3, 1).reshape(-1, page_rows)
    pt_flat = page_table.reshape(-1)
    sel = _kmean_select(pt_flat, heads(q_s), ck2, dbsz, n_pages)
    sel_flat = sel[:, :, :MOBA_TOPK].reshape(-1)
    attn_s = _moba_sample(pt_flat, sel_flat, heads(q_s), heads(k_s), heads(v_s), ck2, cv2, dbsz, n_pages)
    conv_s = _conv_sample(state_conv[l], u_s, conv_w[l], cb, lg, lb)
    y_s = _post_mixer(xs, attn_s.reshape(dbsz, ATTN_WIDTH).astype(BF16), conv_s, w_out_b, g2, w_rt, b_rt,
                      wg_b, wu_b, wd_b, gf, dbsz, "post_mixer_sample")

    hist = CONV_LEN - 1
    kv_p = lambda t: t.reshape(bsz, N_HEADS, HEAD_DIM, slen).transpose(0, 3, 1, 2)[None]
    kv_s = lambda t: t.reshape(1, dbsz, 1, N_HEADS, HEAD_DIM)
    new_conv_p = u_p.reshape(bsz, slen, CONV_CH)[:, slen - hist:][None]
    new_conv_s = jnp.concatenate([state_conv[l][:, 1:], u_s[:, None, :]], axis=1)[None]
    return (y_p.reshape(bsz, slen, D_MODEL), y_s.reshape(dbsz, 1, D_MODEL), kv_p(kt_p), kv_p(vt_p), new_conv_p,
            kv_s(k_s), kv_s(v_s), new_conv_s)
```

```python
import functools
import math

import jax
import jax.numpy as jnp
import numpy as np
from jax import lax
from jax.experimental import pallas as pl
from jax.experimental.pallas import tpu as pltpu

F32 = jnp.float32
BF16 = jnp.bfloat16
HIGHEST = lax.Precision.HIGHEST

D_MODEL = 1024
ATTN_WIDTH = 512
CONV_CH = 512
HEAD_DIM = 64
N_HEADS = 8
IN_WIDTH = 3 * ATTN_WIDTH + 2 * CONV_CH
CONV_LEN = 31
MOBA_BLOCK = 256
MOBA_TOPK = 3
ROPE_THETA = 10000.0
N_EXPERT_GROUPS = 4
EXPERTS_PER_GROUP = 4
N_EXPERTS = 16
D_EXPERT = 512
RMS_EPS = 1e-6
LN_EPS = 1e-5
NEG_INF = -1e30
SM_SCALE = HEAD_DIM ** -0.5
_ROPE_LOG_STEP = -math.log(ROPE_THETA) / (HEAD_DIM // 2)
ROPE_LOG_STEP_HI = float(np.float32(_ROPE_LOG_STEP))
ROPE_LOG_STEP_LO = _ROPE_LOG_STEP - ROPE_LOG_STEP_HI

LANES = 128
SUBLANES = 8
HEADS_PER_LANE_TILE = LANES // HEAD_DIM
ROUTER_COLS = LANES
GROUP_COL0 = N_EXPERTS
VMEM_LIMIT = 56 * 1024 * 1024
CONV_HALO = 32
CONV_CHUNK = 64
LOG2_E = math.log2(math.e)
PAST_SPAN = 4
KMEAN_RING = 8
PAGES_PER_BLOCK = 2
MOE_TILE = 512
GATHER_UNROLL = 8
ROW_WIDTH = D_MODEL + ROUTER_COLS
GROUP_ID_COL = ROUTER_COLS - 1


def _cparams(*sem):
    return pltpu.CompilerParams(dimension_semantics=sem, vmem_limit_bytes=VMEM_LIMIT)


def _inproj_kernel(x_ref, g_ref, w_ref, b_ref, q_ref, k_ref, v_ref, u_ref, *prompt_refs,
                   pos0, pos_stride, n_pos_tiles):
    x = x_ref[...]
    tm = x.shape[0]
    h = x * lax.rsqrt(jnp.mean(x * x, axis=-1, keepdims=True) + RMS_EPS) * g_ref[...]
    z = jnp.dot(h.astype(BF16), w_ref[...], preferred_element_type=F32) + b_ref[...]
    a = ATTN_WIDTH
    reps = a // LANES
    half = HEAD_DIM // 2
    lane_t = lax.broadcasted_iota(jnp.int32, (1, LANES), 1)
    freq_ix = (lane_t % half).astype(F32)
    inv_freq = jnp.exp(freq_ix * ROPE_LOG_STEP_HI) * jnp.exp(freq_ix * ROPE_LOG_STEP_LO)
    tile_pos = pos0 + (pl.program_id(0) % n_pos_tiles) * (tm * pos_stride)
    pos = tile_pos + lax.broadcasted_iota(jnp.int32, (tm, 1), 0) * pos_stride
    ang = pos.astype(F32) * inv_freq
    sin_sign = jnp.where((lane_t % HEAD_DIM) < half, -1.0, 1.0)
    cos = jnp.concatenate([jnp.cos(ang)] * reps, axis=-1)
    sin = jnp.concatenate([jnp.sin(ang) * sin_sign] * reps, axis=-1)
    lane = lax.broadcasted_iota(jnp.int32, (1, a), 1)
    first_half = (lane % HEAD_DIM) < half

    def rope(t):
        from_below = pltpu.roll(t, HEAD_DIM // 2, axis=1)
        from_above = pltpu.roll(t, a - HEAD_DIM // 2, axis=1)
        return t * cos + jnp.where(first_half, from_above, from_below) * sin

    q = rope(z[:, 0:a])
    k = rope(z[:, a:2 * a])
    v = z[:, 2 * a:3 * a]
    u = z[:, 3 * a:3 * a + CONV_CH] * jax.nn.sigmoid(z[:, 3 * a + CONV_CH:])
    q_ref[...] = q
    u_ref[...] = u
    if prompt_refs:
        kb_ref, vtb_ref, km_ref = prompt_refs
        v_t = v.T
        k_ref[...] = k.T
        v_ref[...] = v_t
        kb_ref[...] = k.astype(BF16)
        vtb_ref[...] = v_t.astype(BF16)
        nblk = k.shape[0] // MOBA_BLOCK
        km_ref[0] = jnp.mean(k.reshape(nblk, MOBA_BLOCK, a), axis=1)
    else:
        k_ref[...] = k
        v_ref[...] = v


def _in_projection(x2d, norm_g, w_bf16, b_in, tm, prompt, pos0, pos_stride, seq_len):
    t = x2d.shape[0]
    n_t = t // tm
    kern = functools.partial(_inproj_kernel, pos0=pos0, pos_stride=pos_stride, n_pos_tiles=seq_len // tm)
    row = lambda i: (i, 0)
    fixed = lambda i: (0, 0)
    n_pos = seq_len // tm
    wide = jax.ShapeDtypeStruct((t, ATTN_WIDTH), F32)
    wide_spec = pl.BlockSpec((tm, ATTN_WIDTH), row)
    if prompt:
        nblk = tm // MOBA_BLOCK
        n_seq = t // seq_len
        feat_spec = pl.BlockSpec((ATTN_WIDTH, tm), lambda i: (i // n_pos, i % n_pos))
        feat_f32 = jax.ShapeDtypeStruct((n_seq * ATTN_WIDTH, seq_len), F32)
        feat_bf16 = jax.ShapeDtypeStruct((n_seq * ATTN_WIDTH, seq_len), BF16)
        out_shape = [wide, feat_f32, feat_f32, wide, jax.ShapeDtypeStruct((t, ATTN_WIDTH), BF16), feat_bf16,
                     jax.ShapeDtypeStruct((n_t, nblk, ATTN_WIDTH), F32)]
        out_specs = [wide_spec, feat_spec, feat_spec, wide_spec, wide_spec, feat_spec,
                     pl.BlockSpec((1, nblk, ATTN_WIDTH), lambda i: (i, 0, 0))]
    else:
        out_shape = [wide] * 4
        out_specs = [wide_spec] * 4
    return pl.pallas_call(
        kern,
        grid=(n_t,),
        in_specs=[
            pl.BlockSpec((tm, D_MODEL), row),
            pl.BlockSpec((1, D_MODEL), fixed),
            pl.BlockSpec((D_MODEL, IN_WIDTH), fixed),
            pl.BlockSpec((1, IN_WIDTH), fixed),
        ],
        out_specs=out_specs,
        out_shape=out_shape,
        compiler_params=_cparams("parallel"),
        name="in_projection_prompt" if prompt else "in_projection_sample",
    )(x2d, norm_g, w_bf16, b_in)


def _moba_prompt_kernel(q_ref, kb_ref, vt_ref, km_ref, o_ref, m_sc, l_sc, acc_sc, bias_sc):
    qi = pl.program_id(2)
    blk_rows = MOBA_BLOCK
    q = q_ref[...]
    lane = lax.broadcasted_iota(jnp.int32, (1, LANES), 1)
    head0 = lane < HEAD_DIM
    q2 = jnp.concatenate([jnp.where(head0, q, 0.0), jnp.where(head0, 0.0, q)], axis=0)
    km = km_ref[0]
    n_blocks = km.shape[0]
    nt = (((1,), (1,)), ((), ()))
    gate = lax.dot_general(km, q2, nt, precision=HIGHEST, preferred_element_type=F32)
    blk = lax.broadcasted_iota(jnp.int32, gate.shape, 0)
    fully_past = blk < qi
    gate = jnp.where(fully_past, gate, NEG_INF)
    rank = jnp.zeros(gate.shape, jnp.int32)
    for m in range(n_blocks):
        gm = gate[m:m + 1, :]
        ahead = (gm > gate) | ((gm == gate) & (m < blk))
        rank = rank + ahead.astype(jnp.int32)
    bias_sc[...] = jnp.where(fully_past & (rank < MOBA_TOPK), 0.0, NEG_INF)

    qb = (q2 * (SM_SCALE * LOG2_E)).astype(BF16)

    own0 = pl.multiple_of(qi * blk_rows, blk_rows)
    s = lax.dot_general(kb_ref[pl.ds(own0, blk_rows), :], qb, nt, preferred_element_type=F32)
    k_ix = lax.broadcasted_iota(jnp.int32, s.shape, 0)
    q_ix = lax.broadcasted_iota(jnp.int32, s.shape, 1) % blk_rows
    s = jnp.where(k_ix <= q_ix, s, NEG_INF)
    m0 = jnp.max(s, axis=0, keepdims=True)
    p = jnp.exp2(s - m0)
    m_sc[...] = m0
    l_sc[...] = jnp.sum(p, axis=0, keepdims=True)
    acc_sc[...] = jnp.dot(vt_ref[:, pl.ds(own0, blk_rows)], p.astype(BF16), preferred_element_type=F32)

    def past_blocks(j0, n):
        r0 = pl.multiple_of(j0 * blk_rows, blk_rows)
        rows = n * blk_rows
        s = lax.dot_general(kb_ref[pl.ds(r0, rows), :], qb, nt, preferred_element_type=F32)
        s = jnp.concatenate([s[i * blk_rows:(i + 1) * blk_rows] + bias_sc[pl.ds(j0 + i, 1), :]
                             for i in range(n)], axis=0)
        m_old = m_sc[...]
        m_new = jnp.maximum(m_old, jnp.max(s, axis=0, keepdims=True))
        alpha = jnp.exp2(m_old - m_new)
        p = jnp.exp2(s - m_new)
        l_sc[...] = alpha * l_sc[...] + jnp.sum(p, axis=0, keepdims=True)
        acc_sc[...] = alpha * acc_sc[...] + jnp.dot(vt_ref[:, pl.ds(r0, rows)], p.astype(BF16),
                                                    preferred_element_type=F32)
        m_sc[...] = m_new

    def span_body(i, carry):
        past_blocks(i * PAST_SPAN, PAST_SPAN)
        return carry

    lax.fori_loop(0, qi // PAST_SPAN, span_body, 0)
    done = (qi // PAST_SPAN) * PAST_SPAN
    n = PAST_SPAN // 2
    while n >= 1:
        @pl.when((qi & n) != 0)
        def _(done=done, n=n):
            past_blocks(done, n)
        done = done + (qi & n)
        n //= 2
    out = acc_sc[...] / l_sc[...]
    out_t = jnp.concatenate([out[:HEAD_DIM, :blk_rows], out[HEAD_DIM:, blk_rows:]], axis=0)
    o_ref[...] = out_t.T.astype(o_ref.dtype)


def _moba_prompt(q, kb, vtb, kmean, bsz, slen):
    n_q = slen // MOBA_BLOCK
    n_hp = ATTN_WIDTH // LANES
    cols2 = HEADS_PER_LANE_TILE * MOBA_BLOCK
    return pl.pallas_call(
        _moba_prompt_kernel,
        grid=(bsz, n_hp, n_q),
        in_specs=[
            pl.BlockSpec((MOBA_BLOCK, LANES), lambda b, hp, qi: (b * n_q + qi, hp)),
            pl.BlockSpec((slen, LANES), lambda b, hp, qi: (b, hp)),
            pl.BlockSpec((LANES, slen), lambda b, hp, qi: (b * n_hp + hp, 0)),
            pl.BlockSpec((1, n_q, LANES), lambda b, hp, qi: (b, 0, hp)),
        ],
        out_specs=pl.BlockSpec((MOBA_BLOCK, LANES), lambda b, hp, qi: (b * n_q + qi, hp)),
        out_shape=jax.ShapeDtypeStruct((bsz * slen, ATTN_WIDTH), BF16),
        scratch_shapes=[
            pltpu.VMEM((1, cols2), F32),
            pltpu.VMEM((1, cols2), F32),
            pltpu.VMEM((LANES, cols2), F32),
            pltpu.VMEM((n_q, cols2), F32),
        ],
        compiler_params=_cparams("parallel", "parallel", "arbitrary"),
        name="moba_prompt",
    )(q, kb, vtb, kmean)


def _ln_swish(c, g, b):
    mu = jnp.mean(c, axis=-1, keepdims=True)
    d = c - mu
    var = jnp.mean(d * d, axis=-1, keepdims=True)
    y = d * lax.rsqrt(var + LN_EPS) * g + b
    return y * jax.nn.sigmoid(y)


def _conv_prompt_kernel(prev_ref, cur_ref, w_ref, cb_ref, g_ref, b_ref, o_ref, ext_sc):
    i = pl.program_id(1)
    tc = cur_ref.shape[0]
    ext_sc[0:CONV_HALO, :] = jnp.where(i > 0, prev_ref[...], 0.0)
    ext_sc[CONV_HALO:CONV_HALO + tc, :] = cur_ref[...]
    ext_sc[CONV_HALO + tc:CONV_HALO + tc + SUBLANES, :] = jnp.zeros((SUBLANES, CONV_CH), F32)
    lead = CONV_HALO - (CONV_LEN - 1)
    for r0 in range(0, tc, CONV_CHUNK):
        acc = jnp.zeros((CONV_CHUNK, CONV_CH), F32)
        for res in range(SUBLANES):
            taps = [j for j in range(CONV_LEN) if (j + lead) % SUBLANES == res]
            part = None
            for j in taps:
                base = r0 + j + lead - res
                term = w_ref[j:j + 1, :] * ext_sc[base:base + CONV_CHUNK + SUBLANES, :]
                part = term if part is None else part + term
            if part is not None:
                acc = acc + part[res:res + CONV_CHUNK]
        y = _ln_swish(acc + cb_ref[...], g_ref[...], b_ref[...])
        o_ref[r0:r0 + CONV_CHUNK, :] = y.astype(o_ref.dtype)


def _conv_prompt(u2d, conv_w, conv_b, ln_g, ln_b, bsz, slen, tc):
    n_c = slen // tc
    halo_per_tile = tc // CONV_HALO
    fixed = lambda b, i: (0, 0)
    return pl.pallas_call(
        _conv_prompt_kernel,
        grid=(bsz, n_c),
        in_specs=[
            pl.BlockSpec((CONV_HALO, CONV_CH),
                         lambda b, i: (jnp.maximum((b * n_c + i) * halo_per_tile - 1, 0), 0)),
            pl.BlockSpec((tc, CONV_CH), lambda b, i: (b * n_c + i, 0)),
            pl.BlockSpec((CONV_LEN, CONV_CH), fixed),
            pl.BlockSpec((1, CONV_CH), fixed),
            pl.BlockSpec((1, CONV_CH), fixed),
            pl.BlockSpec((1, CONV_CH), fixed),
        ],
        out_specs=pl.BlockSpec((tc, CONV_CH), lambda b, i: (b * n_c + i, 0)),
        out_shape=jax.ShapeDtypeStruct((bsz * slen, CONV_CH), BF16),
        scratch_shapes=[pltpu.VMEM((CONV_HALO + tc + SUBLANES, CONV_CH), F32)],
        compiler_params=_cparams("parallel", "parallel"),
        name="conv_prompt",
    )(u2d, u2d, conv_w, conv_b, ln_g, ln_b)


def _conv_sample_kernel(st_ref, u_ref, w_ref, cb_ref, g_ref, b_ref, o_ref):
    hist = CONV_LEN - 1
    for b in range(st_ref.shape[0]):
        u_row = u_ref[b:b + 1, :]
        c = jnp.sum(st_ref[b] * w_ref[0:hist, :], axis=0, keepdims=True) + w_ref[hist:hist + 1, :] * u_row
        y = _ln_swish(c + cb_ref[...], g_ref[...], b_ref[...])
        o_ref[b:b + 1, :] = y.astype(o_ref.dtype)


def _conv_sample(state, u2d, conv_w, conv_b, ln_g, ln_b):
    n = u2d.shape[0]
    return pl.pallas_call(
        _conv_sample_kernel,
        out_shape=jax.ShapeDtypeStruct((n, CONV_CH), BF16),
        compiler_params=pltpu.CompilerParams(vmem_limit_bytes=VMEM_LIMIT),
        name="conv_sample",
    )(state, u2d, conv_w, conv_b, ln_g, ln_b)


def _route(logits, with_group=False):
    c = lax.broadcasted_iota(jnp.int32, logits.shape, 1)
    big = jnp.int32(ROUTER_COLS)
    is_group = (c >= GROUP_COL0) & (c < GROUP_COL0 + N_EXPERT_GROUPS)
    gl = jnp.where(is_group, logits, NEG_INF)
    gmax = jnp.max(gl, axis=1, keepdims=True)
    g_sel = jnp.min(jnp.where(gl == gmax, c - GROUP_COL0, big), axis=1, keepdims=True)
    g_w = 1.0 / jnp.sum(jnp.exp(gl - gmax), axis=1, keepdims=True)
    in_group = (c < N_EXPERTS) & ((c // EXPERTS_PER_GROUP) == g_sel)
    el = jnp.where(in_group, logits, NEG_INF)
    m1 = jnp.max(el, axis=1, keepdims=True)
    i1 = jnp.min(jnp.where(el == m1, c, big), axis=1, keepdims=True)
    el2 = jnp.where(c == i1, NEG_INF, el)
    m2 = jnp.max(el2, axis=1, keepdims=True)
    i2 = jnp.min(jnp.where(el2 == m2, c, big), axis=1, keepdims=True)
    e21 = jnp.exp(m2 - m1)
    inv = 1.0 / (1.0 + e21)
    comb = g_w * jnp.where(c == i1, inv, jnp.where(c == i2, e21 * inv, 0.0))
    return (comb, g_sel) if with_group else comb


def _post_kernel(x_ref, a_ref, c_ref, wo_ref, g2_ref, wr_ref, br_ref, wg_ref, wu_ref, wd_ref, gf_ref,
                 y_ref, x1_sc, h2_sc, comb_sc, acc_sc):
    e = pl.program_id(1)

    @pl.when(e == 0)
    def _():
        x1 = (x_ref[...]
              + jnp.dot(a_ref[...], wo_ref[0:ATTN_WIDTH, :], preferred_element_type=F32)
              + jnp.dot(c_ref[...], wo_ref[ATTN_WIDTH:D_MODEL, :], preferred_element_type=F32))
        x1_sc[...] = x1
        h2 = x1 * lax.rsqrt(jnp.mean(x1 * x1, axis=-1, keepdims=True) + RMS_EPS) * g2_ref[...]
        h2_sc[...] = h2.astype(BF16)
        logits = jnp.dot(h2, wr_ref[...], precision=HIGHEST, preferred_element_type=F32) + br_ref[...]
        comb_sc[...] = _route(logits)
        acc_sc[...] = jnp.zeros_like(acc_sc)

    h2b = h2_sc[...]
    col = lax.broadcasted_iota(jnp.int32, comb_sc.shape, 1)
    ce = jnp.sum(jnp.where(col == e, comb_sc[...], 0.0), axis=1, keepdims=True)
    gt = jnp.dot(h2b, wg_ref[0], preferred_element_type=F32)
    up = jnp.dot(h2b, wu_ref[0], preferred_element_type=F32)
    hid = gt * jax.nn.sigmoid(gt) * up * ce
    acc_sc[...] += jnp.dot(hid.astype(BF16), wd_ref[0], preferred_element_type=F32)

    @pl.when(e == N_EXPERTS - 1)
    def _():
        x2 = x1_sc[...] + acc_sc[...]
        y_ref[...] = x2 * lax.rsqrt(jnp.mean(x2 * x2, axis=-1, keepdims=True) + RMS_EPS) * gf_ref[...]


def _post_mixer(x2d, attn, conv, w_out, g2, w_rt, b_rt, wg, wu, wd, gf, tm, name):
    t = x2d.shape[0]
    row = lambda i, e: (i, 0)
    fixed = lambda i, e: (0, 0)
    return pl.pallas_call(
        _post_kernel,
        grid=(t // tm, N_EXPERTS),
        in_specs=[
            pl.BlockSpec((tm, D_MODEL), row),
            pl.BlockSpec((tm, ATTN_WIDTH), row),
            pl.BlockSpec((tm, CONV_CH), row),
            pl.BlockSpec((D_MODEL, D_MODEL), fixed),
            pl.BlockSpec((1, D_MODEL), fixed),
            pl.BlockSpec((D_MODEL, ROUTER_COLS), fixed),
            pl.BlockSpec((1, ROUTER_COLS), fixed),
            pl.BlockSpec((1, D_MODEL, D_EXPERT), lambda i, e: (e, 0, 0)),
            pl.BlockSpec((1, D_MODEL, D_EXPERT), lambda i, e: (e, 0, 0)),
            pl.BlockSpec((1, D_EXPERT, D_MODEL), lambda i, e: (e, 0, 0)),
            pl.BlockSpec((1, D_MODEL), fixed),
        ],
        out_specs=pl.BlockSpec((tm, D_MODEL), row),
        out_shape=jax.ShapeDtypeStruct((t, D_MODEL), F32),
        scratch_shapes=[
            pltpu.VMEM((tm, D_MODEL), F32),
            pltpu.VMEM((tm, D_MODEL), BF16),
            pltpu.VMEM((tm, ROUTER_COLS), F32),
            pltpu.VMEM((tm, D_MODEL), F32),
        ],
        compiler_params=_cparams("parallel", "arbitrary"),
        name=name,
    )(x2d, attn, conv, w_out, g2, w_rt, b_rt, wg, wu, wd, gf)


def _mix_route_kernel(x_ref, a_ref, c_ref, wo_ref, g2_ref, wr_ref, br_ref, x1_ref, row_ref):
    x1 = (x_ref[...]
          + jnp.dot(a_ref[...], wo_ref[0:ATTN_WIDTH, :], preferred_element_type=F32)
          + jnp.dot(c_ref[...], wo_ref[ATTN_WIDTH:D_MODEL, :], preferred_element_type=F32))
    x1_ref[...] = x1
    h2 = x1 * lax.rsqrt(jnp.mean(x1 * x1, axis=-1, keepdims=True) + RMS_EPS) * g2_ref[...]
    logits = jnp.dot(h2, wr_ref[...], precision=HIGHEST, preferred_element_type=F32) + br_ref[...]
    comb, g_sel = _route(logits, with_group=True)
    col = lax.broadcasted_iota(jnp.int32, comb.shape, 1)
    row_ref[:, 0:D_MODEL] = h2
    row_ref[:, D_MODEL:ROW_WIDTH] = jnp.where(col == GROUP_ID_COL, g_sel.astype(F32), comb)


def _mix_route(x2d, attn, conv, w_out, g2, w_rt, b_rt, tm):
    t = x2d.shape[0]
    row = lambda i: (i, 0)
    fixed = lambda i: (0, 0)
    return pl.pallas_call(
        _mix_route_kernel,
        grid=(t // tm,),
        in_specs=[
            pl.BlockSpec((tm, D_MODEL), row),
            pl.BlockSpec((tm, ATTN_WIDTH), row),
            pl.BlockSpec((tm, CONV_CH), row),
            pl.BlockSpec((D_MODEL, D_MODEL), fixed),
            pl.BlockSpec((1, D_MODEL), fixed),
            pl.BlockSpec((D_MODEL, ROUTER_COLS), fixed),
            pl.BlockSpec((1, ROUTER_COLS), fixed),
        ],
        out_specs=[pl.BlockSpec((tm, D_MODEL), row), pl.BlockSpec((tm, ROW_WIDTH), row)],
        out_shape=[jax.ShapeDtypeStruct((t, D_MODEL), F32), jax.ShapeDtypeStruct((t, ROW_WIDTH), F32)],
        compiler_params=_cparams("parallel"),
        name="mix_route_prompt",
    )(x2d, attn, conv, w_out, g2, w_rt, b_rt)


def _gather_rows(idx_ref, base, src_hbm, dst, sem, n_rows):
    def body(c, carry):
        for k in range(GATHER_UNROLL):
            r = c * GATHER_UNROLL + k
            pltpu.make_async_copy(src_hbm.at[pl.ds(idx_ref[base + r], 1), :], dst.at[pl.ds(r, 1), :], sem).start()
        return carry
    lax.fori_loop(0, n_rows // GATHER_UNROLL, body, 0)


def _wait_rows(src_hbm, dst, sem, n_rows):
    def body(c, carry):
        for k in range(GATHER_UNROLL):
            r = c * GATHER_UNROLL + k
            pltpu.make_async_copy(src_hbm.at[pl.ds(0, 1), :], dst.at[pl.ds(r, 1), :], sem).wait()
        return carry
    lax.fori_loop(0, n_rows // GATHER_UNROLL, body, 0)


def _moe_grouped_kernel(tg_ref, src_ref, rows_hbm, wg_ref, wu_ref, wd_ref, y_ref, xbuf, sem, hb_sc, acc_sc,
                        *, n_tiles):
    i = pl.program_id(0)
    j = pl.program_id(1)
    slot = i % 2
    group = tg_ref[i]

    @pl.when(j == 0)
    def _():
        @pl.when(i == 0)
        def _():
            _gather_rows(src_ref, 0, rows_hbm, xbuf.at[0], sem.at[0], MOE_TILE)

        @pl.when(i + 1 < n_tiles)
        def _():
            _gather_rows(src_ref, (i + 1) * MOE_TILE, rows_hbm, xbuf.at[1 - slot], sem.at[1 - slot], MOE_TILE)

        _wait_rows(rows_hbm, xbuf.at[slot], sem.at[slot], MOE_TILE)
        hb_sc[...] = xbuf[slot, :, 0:D_MODEL].astype(BF16)
        acc_sc[...] = jnp.zeros_like(acc_sc)

    @pl.when(group < N_EXPERT_GROUPS)
    def _():
        e = group * EXPERTS_PER_GROUP + j
        comb = xbuf[slot, :, D_MODEL:ROW_WIDTH]
        col = lax.broadcasted_iota(jnp.int32, comb.shape, 1)
        ce = jnp.sum(jnp.where(col == e, comb, 0.0), axis=1, keepdims=True)
        hb = hb_sc[...]
        gt = jnp.dot(hb, wg_ref[0], preferred_element_type=F32)
        up = jnp.dot(hb, wu_ref[0], preferred_element_type=F32)
        hid = gt * jax.nn.sigmoid(gt) * up * ce
        acc_sc[...] += jnp.dot(hid.astype(BF16), wd_ref[0], preferred_element_type=F32)

    @pl.when(j == EXPERTS_PER_GROUP - 1)
    def _():
        y_ref[...] = acc_sc[...]


def _moe_grouped(tile_group, src_of_sorted, rows, wg, wu, wd, n_tiles):
    kern = functools.partial(_moe_grouped_kernel, n_tiles=n_tiles)
    last = N_EXPERT_GROUPS - 1
    w_ix = lambda i, j, tg, src: (jnp.minimum(tg[i], last) * EXPERTS_PER_GROUP + j, 0, 0)
    return pl.pallas_call(
        kern,
        grid_spec=pltpu.PrefetchScalarGridSpec(
            num_scalar_prefetch=2,
            grid=(n_tiles, EXPERTS_PER_GROUP),
            in_specs=[
                pl.BlockSpec(memory_space=pl.ANY),
                pl.BlockSpec((1, D_MODEL, D_EXPERT), w_ix),
                pl.BlockSpec((1, D_MODEL, D_EXPERT), w_ix),
                pl.BlockSpec((1, D_EXPERT, D_MODEL), w_ix),
            ],
            out_specs=pl.BlockSpec((MOE_TILE, D_MODEL), lambda i, j, tg, src: (i, 0)),
            scratch_shapes=[
                pltpu.VMEM((2, MOE_TILE, ROW_WIDTH), F32),
                pltpu.SemaphoreType.DMA((2,)),
                pltpu.VMEM((MOE_TILE, D_MODEL), BF16),
                pltpu.VMEM((MOE_TILE, D_MODEL), F32),
            ],
        ),
        out_shape=jax.ShapeDtypeStruct((n_tiles * MOE_TILE, D_MODEL), F32),
        compiler_params=_cparams("arbitrary", "arbitrary"),
        name="moe_grouped_prompt",
    )(tile_group, src_of_sorted, rows, wg, wu, wd)


def _moe_finish_kernel(pos_ref, x1_ref, gf_ref, ys_hbm, y_ref, buf, sem, *, n_tiles):
    i = pl.program_id(0)
    slot = i % 2
    tm = x1_ref.shape[0]

    @pl.when(i == 0)
    def _():
        _gather_rows(pos_ref, 0, ys_hbm, buf.at[0], sem.at[0], tm)

    @pl.when(i + 1 < n_tiles)
    def _():
        _gather_rows(pos_ref, (i + 1) * tm, ys_hbm, buf.at[1 - slot], sem.at[1 - slot], tm)

    _wait_rows(ys_hbm, buf.at[slot], sem.at[slot], tm)
    x2 = x1_ref[...] + buf[slot]
    y_ref[...] = x2 * lax.rsqrt(jnp.mean(x2 * x2, axis=-1, keepdims=True) + RMS_EPS) * gf_ref[...]


def _moe_finish(pos_of_token, x1, gf, ys, tm):
    t = x1.shape[0]
    n_tiles = t // tm
    kern = functools.partial(_moe_finish_kernel, n_tiles=n_tiles)
    return pl.pallas_call(
        kern,
        grid_spec=pltpu.PrefetchScalarGridSpec(
            num_scalar_prefetch=1,
            grid=(n_tiles,),
            in_specs=[
                pl.BlockSpec((tm, D_MODEL), lambda i, pos: (i, 0)),
                pl.BlockSpec((1, D_MODEL), lambda i, pos: (0, 0)),
                pl.BlockSpec(memory_space=pl.ANY),
            ],
            out_specs=pl.BlockSpec((tm, D_MODEL), lambda i, pos: (i, 0)),
            scratch_shapes=[pltpu.VMEM((2, tm, D_MODEL), F32), pltpu.SemaphoreType.DMA((2,))],
        ),
        out_shape=jax.ShapeDtypeStruct((t, D_MODEL), F32),
        compiler_params=_cparams("arbitrary"),
        name="moe_finish_prompt",
    )(pos_of_token, x1, gf, ys)


def _sort_plan(group_id, n_tiles):
    t = group_id.shape[0]
    onehot = (group_id[:, None] == jnp.arange(N_EXPERT_GROUPS, dtype=jnp.int32)[None, :]).astype(jnp.int32)
    counts = jnp.sum(onehot, axis=0)
    rank = jnp.sum((jnp.cumsum(onehot, axis=0) - onehot) * onehot, axis=1)
    padded = (counts + MOE_TILE - 1) // MOE_TILE * MOE_TILE
    group_end = jnp.cumsum(padded)
    group_start = group_end - padded
    pos = group_start[group_id] + rank
    src = jnp.zeros((n_tiles * MOE_TILE,), jnp.int32).at[pos].set(jnp.arange(t, dtype=jnp.int32))
    tile_start = jnp.arange(n_tiles, dtype=jnp.int32) * MOE_TILE
    tile_group = jnp.sum((tile_start[:, None] >= group_end[None, :]).astype(jnp.int32), axis=1)
    return pos.astype(jnp.int32), src, tile_group.astype(jnp.int32)


def _post_mixer_grouped(x2d, attn, conv, w_out_b, g2, w_rt, b_rt, wg_b, wu_b, wd_b, gf):
    t = x2d.shape[0]
    n_tiles = t // MOE_TILE + N_EXPERT_GROUPS
    x1, rows = _mix_route(x2d, attn, conv, w_out_b, g2, w_rt, b_rt, MOE_TILE)
    group_id = rows[:, D_MODEL + GROUP_ID_COL].astype(jnp.int32)
    pos, src, tile_group = _sort_plan(group_id, n_tiles)
    ys = _moe_grouped(tile_group, src, rows, wg_b, wu_b, wd_b, n_tiles)
    return _moe_finish(pos, x1, gf, ys, MOE_TILE)


def _kmean_select_kernel(pt_ref, q_ref, ck_hbm, sel_ref, buf, sem, sums_sc, *, n_pages):
    b = pl.program_id(0)
    n_blocks = n_pages // PAGES_PER_BLOCK
    page_flat = ATTN_WIDTH

    def page_copy(p, slot):
        start = pl.multiple_of(pt_ref[b * n_pages + p] * page_flat, page_flat)
        return pltpu.make_async_copy(ck_hbm.at[pl.ds(start, page_flat), :], buf.at[slot], sem.at[slot])

    for s in range(KMEAN_RING):
        page_copy(s, s).start()
    sums_sc[...] = jnp.zeros_like(sums_sc)
    lane = lax.broadcasted_iota(jnp.int32, (1, LANES), 1)
    sub = SUBLANES
    groups = HEAD_DIM // sub

    def block_body(n, carry):
        part = jnp.zeros((N_HEADS, sub, LANES), F32)
        for pg in range(PAGES_PER_BLOCK):
            p = n * PAGES_PER_BLOCK + pg
            slot = p % KMEAN_RING
            page_copy(p, slot).wait()
            prod = buf[slot] * q_ref[0]
            part = part + jnp.sum(prod.reshape(N_HEADS, groups, sub, LANES), axis=1)

            @pl.when(p + KMEAN_RING < n_pages)
            def _():
                page_copy(p + KMEAN_RING, slot).start()
        block_sum = jnp.sum(part.reshape(N_HEADS * sub, LANES), axis=1, keepdims=True)
        sums_sc[...] = jnp.where(lane == n, block_sum, sums_sc[...])
        return carry

    lax.fori_loop(0, n_blocks, block_body, 0)

    fold = (lax.broadcasted_iota(jnp.int32, (N_HEADS, N_HEADS * sub), 1) // sub
            == lax.broadcasted_iota(jnp.int32, (N_HEADS, N_HEADS * sub), 0)).astype(F32)
    gate = jnp.dot(fold, sums_sc[...], precision=HIGHEST, preferred_element_type=F32) * (1.0 / MOBA_BLOCK)
    blk = lax.broadcasted_iota(jnp.int32, gate.shape, 1)
    gate = jnp.where(blk < n_blocks, gate, NEG_INF)
    rank = jnp.zeros(gate.shape, jnp.int32)
    for s in range(1, LANES):
        other = pltpu.roll(gate, s, axis=1)
        other_blk = jnp.where(blk >= s, blk - s, blk - s + LANES)
        ahead = (other > gate) | ((other == gate) & (other_blk < blk))
        rank = rank + ahead.astype(jnp.int32)
    out = jnp.zeros(gate.shape, jnp.int32)
    for r in range(MOBA_TOPK):
        picked = jnp.sum(jnp.where(rank == r, blk, 0).astype(F32), axis=1, keepdims=True)
        out = jnp.where(blk == r, picked.astype(jnp.int32), out)
    sel_ref[0] = out


def _kmean_select(page_table_flat, q_lanes, cache_kt, n_seq, n_pages):
    kern = functools.partial(_kmean_select_kernel, n_pages=n_pages)
    return pl.pallas_call(
        kern,
        grid_spec=pltpu.PrefetchScalarGridSpec(
            num_scalar_prefetch=1,
            grid=(n_seq,),
            in_specs=[
                pl.BlockSpec((1, ATTN_WIDTH, LANES), lambda b, pt: (b, 0, 0)),
                pl.BlockSpec(memory_space=pl.ANY),
            ],
            out_specs=pl.BlockSpec((1, N_HEADS, LANES), lambda b, pt: (b, 0, 0)),
            scratch_shapes=[
                pltpu.VMEM((KMEAN_RING, ATTN_WIDTH, LANES), F32),
                pltpu.SemaphoreType.DMA((KMEAN_RING,)),
                pltpu.VMEM((N_HEADS * SUBLANES, LANES), F32),
            ],
        ),
        out_shape=jax.ShapeDtypeStruct((n_seq, N_HEADS, LANES), jnp.int32),
        compiler_params=_cparams("arbitrary"),
        name="kmean_select_sample",
    )(page_table_flat, q_lanes, cache_kt)


def _moba_sample_kernel(pt_ref, sel_ref, q_ref, kn_ref, vn_ref, ck_hbm, cv_hbm, o_ref,
                        kbuf, vbuf, ksem, vsem, *, n_pages, n_seq):
    b = pl.program_id(0)
    half = b % 2

    def seq_copies(seq, dst_half):
        copies = []
        for h in range(N_HEADS):
            for r in range(MOBA_TOPK):
                blk = sel_ref[(seq * N_HEADS + h) * MOBA_TOPK + r]
                for pg in range(PAGES_PER_BLOCK):
                    page = pt_ref[seq * n_pages + blk * PAGES_PER_BLOCK + pg]
                    src = pl.ds(pl.multiple_of(page * ATTN_WIDTH + h * HEAD_DIM, HEAD_DIM), HEAD_DIM)
                    dst = pl.ds((r * PAGES_PER_BLOCK + pg) * LANES, LANES)
                    copies.append(pltpu.make_async_copy(ck_hbm.at[src, :], kbuf.at[dst_half, h, :, dst],
                                                        ksem.at[dst_half]))
                    copies.append(pltpu.make_async_copy(cv_hbm.at[src, :], vbuf.at[dst_half, h, :, dst],
                                                        vsem.at[dst_half]))
        return copies

    @pl.when(b == 0)
    def _():
        for cp in seq_copies(0, 0):
            cp.start()

    @pl.when(b + 1 < n_seq)
    def _():
        for cp in seq_copies(b + 1, 1 - half):
            cp.start()

    for cp in seq_copies(b, half):
        cp.wait()

    q8 = q_ref[0]
    kn = kn_ref[0]
    vn = vn_ref[0]
    nt = (((1,), (1,)), ((), ()))
    for h in range(N_HEADS):
        qh = q8[h:h + 1, :]
        l_past = jnp.dot(qh, kbuf[half, h], precision=HIGHEST, preferred_element_type=F32) * SM_SCALE
        l_self = jnp.sum(qh * kn[h:h + 1, :], axis=1, keepdims=True) * SM_SCALE
        m = jnp.maximum(jnp.max(l_past, axis=1, keepdims=True), l_self)
        p = jnp.exp(l_past - m)
        p_self = jnp.exp(l_self - m)
        den = jnp.sum(p, axis=1, keepdims=True) + p_self
        o = p_self * vn[h:h + 1, :] + lax.dot_general(p, vbuf[half, h], nt, precision=HIGHEST,
                                                      preferred_element_type=F32)
        o_ref[0, h:h + 1, :] = o / den


def _moba_sample(page_table_flat, sel_flat, q3, k3, v3, cache_kt, cache_vt, n_seq, n_pages):
    kern = functools.partial(_moba_sample_kernel, n_pages=n_pages, n_seq=n_seq)
    n_keys = MOBA_TOPK * MOBA_BLOCK
    tok = pl.BlockSpec((1, N_HEADS, HEAD_DIM), lambda b, pt, sel: (b, 0, 0))
    return pl.pallas_call(
        kern,
        grid_spec=pltpu.PrefetchScalarGridSpec(
            num_scalar_prefetch=2,
            grid=(n_seq,),
            in_specs=[tok, tok, tok, pl.BlockSpec(memory_space=pl.ANY), pl.BlockSpec(memory_space=pl.ANY)],
            out_specs=tok,
            scratch_shapes=[
                pltpu.VMEM((2, N_HEADS, HEAD_DIM, n_keys), F32),
                pltpu.VMEM((2, N_HEADS, HEAD_DIM, n_keys), F32),
                pltpu.SemaphoreType.DMA((2,)),
                pltpu.SemaphoreType.DMA((2,)),
            ],
        ),
        out_shape=jax.ShapeDtypeStruct((n_seq, N_HEADS, HEAD_DIM), F32),
        compiler_params=_cparams("arbitrary"),
        name="moba_sample",
    )(page_table_flat, sel_flat, q3, k3, v3, cache_kt, cache_vt)


def _token_tile(t, want):
    return want if t % want == 0 else t


def kernel(x_prompt, x_sample, cache_k, cache_v, state_conv, page_table, norm1_g, w_in, b_in, conv_w, conv_b,
           conv_ln_g, conv_ln_b, w_out, norm2_g, w_group, b_group, w_router, b_router, w_gate, w_up, w_down,
           norm_f_g):
    bsz, slen, _ = x_prompt.shape
    dbsz, dlen, _ = x_sample.shape
    depth = w_in.shape[0]
    assert depth == 1 and dlen == 1, "one layer and one new sample token per sequence"
    n_pages, page_rows = page_table.shape[1], cache_k.shape[2]
    assert MOBA_BLOCK == PAGES_PER_BLOCK * page_rows and slen % MOBA_BLOCK == 0
    assert page_rows == LANES and MOBA_TOPK <= n_pages // PAGES_PER_BLOCK <= LANES and n_pages >= KMEAN_RING
    past_len = n_pages * page_rows
    l = 0

    w_in_b = w_in[l].astype(BF16)
    w_out_b = w_out[l].astype(BF16)
    wg_b, wu_b, wd_b = w_gate[l].astype(BF16), w_up[l].astype(BF16), w_down[l].astype(BF16)
    pad_cols = ROUTER_COLS - N_EXPERTS - N_EXPERT_GROUPS
    w_rt = jnp.pad(jnp.concatenate([w_router[l], w_group[l]], axis=1), ((0, 0), (0, pad_cols)))
    b_rt = jnp.pad(jnp.concatenate([b_router[l], b_group[l]])[None, :], ((0, 0), (0, pad_cols)))
    g1, g2, gf = norm1_g[l][None, :], norm2_g[l][None, :], norm_f_g[None, :]
    b_in2, cb = b_in[l][None, :], conv_b[l][None, :]
    lg, lb = conv_ln_g[l][None, :], conv_ln_b[l][None, :]

    t_p = bsz * slen
    xp = x_prompt.reshape(t_p, D_MODEL)
    tm_in = _token_tile(slen, 512)
    q_p, kt_p, vt_p, u_p, kb_p, vtb_p, km_p = _in_projection(xp, g1, w_in_b, b_in2, tm_in, True, 0, 1, slen)
    km_p = km_p.reshape(bsz, slen // MOBA_BLOCK, ATTN_WIDTH)
    attn_p = _moba_prompt(q_p, kb_p, vtb_p, km_p, bsz, slen)
    conv_p = _conv_prompt(u_p, conv_w[l], cb, lg, lb, bsz, slen, _token_tile(slen, 256))
    y_p = _post_mixer_grouped(xp, attn_p, conv_p, w_out_b, g2, w_rt, b_rt, wg_b, wu_b, wd_b, gf)

    xs = x_sample.reshape(dbsz, D_MODEL)
    q_s, k_s, v_s, u_s = _in_projection(xs, g1, w_in_b, b_in2, dbsz, False, past_len, 0, dbsz)
    heads = lambda t: t.reshape(dbsz, N_HEADS, HEAD_DIM)
    ck2 = cache_k[l].transpose(0, 2, 3, 1).reshape(-1, page_rows)
    cv2 = cache_v[l].transpose(0, 2, 3, 1).reshape(-1, page_rows)
    pt_flat = page_table.reshape(-1)
    q_lanes = jnp.broadcast_to(q_s[:, :, None], (dbsz, ATTN_WIDTH, LANES))
    sel = _kmean_select(pt_flat, q_lanes, ck2, dbsz, n_pages)
    sel_flat = sel[:, :, :MOBA_TOPK].reshape(-1)
    attn_s = _moba_sample(pt_flat, sel_flat, heads(q_s), heads(k_s), heads(v_s), ck2, cv2, dbsz, n_pages)
    conv_s = _conv_sample(state_conv[l], u_s, conv_w[l], cb, lg, lb)
    y_s = _post_mixer(xs, attn_s.reshape(dbsz, ATTN_WIDTH).astype(BF16), conv_s, w_out_b, g2, w_rt, b_rt,
                      wg_b, wu_b, wd_b, gf, dbsz, "post_mixer_sample")

    hist = CONV_LEN - 1
    kv_p = lambda t: t.reshape(bsz, N_HEADS, HEAD_DIM, slen).transpose(0, 3, 1, 2)[None]
    kv_s = lambda t: t.reshape(1, dbsz, 1, N_HEADS, HEAD_DIM)
    new_conv_p = u_p.reshape(bsz, slen, CONV_CH)[:, slen - hist:][None]
    new_conv_s = jnp.concatenate([state_conv[l][:, 1:], u_s[:, None, :]], axis=1)[None]
    return (y_p.reshape(bsz, slen, D_MODEL), y_s.reshape(dbsz, 1, D_MODEL), kv_p(kt_p), kv_p(vt_p), new_conv_p,
            kv_s(k_s), kv_s(v_s), new_conv_s)
```

```python
import functools
import math

import jax
import jax.numpy as jnp
import numpy as np
from jax import lax
from jax.experimental import pallas as pl
from jax.experimental.pallas import tpu as pltpu

F32 = jnp.float32
BF16 = jnp.bfloat16
HIGHEST = lax.Precision.HIGHEST

D_MODEL = 1024
ATTN_WIDTH = 512
CONV_CH = 512
HEAD_DIM = 64
N_HEADS = 8
IN_WIDTH = 3 * ATTN_WIDTH + 2 * CONV_CH
CONV_LEN = 31
MOBA_BLOCK = 256
MOBA_TOPK = 3
ROPE_THETA = 10000.0
N_EXPERT_GROUPS = 4
EXPERTS_PER_GROUP = 4
N_EXPERTS = 16
D_EXPERT = 512
RMS_EPS = 1e-6
LN_EPS = 1e-5
NEG_INF = -1e30
SM_SCALE = HEAD_DIM ** -0.5
_ROPE_LOG_STEP = -math.log(ROPE_THETA) / (HEAD_DIM // 2)
ROPE_LOG_STEP_HI = float(np.float32(_ROPE_LOG_STEP))
ROPE_LOG_STEP_LO = _ROPE_LOG_STEP - ROPE_LOG_STEP_HI

LANES = 128
SUBLANES = 8
HEADS_PER_LANE_TILE = LANES // HEAD_DIM
ROUTER_COLS = LANES
GROUP_COL0 = N_EXPERTS
VMEM_LIMIT = 56 * 1024 * 1024
CONV_HALO = 32
CONV_CHUNK = 64
LOG2_E = math.log2(math.e)
PAST_SPAN = 4
KMEAN_RING = 16
PAGES_PER_BLOCK = 2
MOE_TILE = 512
GATHER_UNROLL = 8
ROW_WIDTH = D_MODEL + ROUTER_COLS
GROUP_ID_COL = ROUTER_COLS - 1


def _cparams(*sem):
    return pltpu.CompilerParams(dimension_semantics=sem, vmem_limit_bytes=VMEM_LIMIT)


def _inproj_kernel(x_ref, g_ref, w_ref, b_ref, q_ref, k_ref, v_ref, u_ref, *prompt_refs,
                   pos0, pos_stride, n_pos_tiles):
    x = x_ref[...]
    tm = x.shape[0]
    h = x * lax.rsqrt(jnp.mean(x * x, axis=-1, keepdims=True) + RMS_EPS) * g_ref[...]
    z = jnp.dot(h.astype(BF16), w_ref[...], preferred_element_type=F32) + b_ref[...]
    a = ATTN_WIDTH
    reps = a // LANES
    half = HEAD_DIM // 2
    lane_t = lax.broadcasted_iota(jnp.int32, (1, LANES), 1)
    freq_ix = (lane_t % half).astype(F32)
    inv_freq = jnp.exp(freq_ix * ROPE_LOG_STEP_HI) * jnp.exp(freq_ix * ROPE_LOG_STEP_LO)
    tile_pos = pos0 + (pl.program_id(0) % n_pos_tiles) * (tm * pos_stride)
    pos = tile_pos + lax.broadcasted_iota(jnp.int32, (tm, 1), 0) * pos_stride
    ang = pos.astype(F32) * inv_freq
    sin_sign = jnp.where((lane_t % HEAD_DIM) < half, -1.0, 1.0)
    cos = jnp.concatenate([jnp.cos(ang)] * reps, axis=-1)
    sin = jnp.concatenate([jnp.sin(ang) * sin_sign] * reps, axis=-1)
    lane = lax.broadcasted_iota(jnp.int32, (1, a), 1)
    first_half = (lane % HEAD_DIM) < half

    def rope(t):
        from_below = pltpu.roll(t, HEAD_DIM // 2, axis=1)
        from_above = pltpu.roll(t, a - HEAD_DIM // 2, axis=1)
        return t * cos + jnp.where(first_half, from_above, from_below) * sin

    q = rope(z[:, 0:a])
    k = rope(z[:, a:2 * a])
    v = z[:, 2 * a:3 * a]
    u = z[:, 3 * a:3 * a + CONV_CH] * jax.nn.sigmoid(z[:, 3 * a + CONV_CH:])
    q_ref[...] = q
    u_ref[...] = u
    if prompt_refs:
        kb_ref, vtb_ref, km_ref = prompt_refs
        v_t = v.T
        k_ref[...] = k.T
        v_ref[...] = v_t
        kb_ref[...] = k.astype(BF16)
        vtb_ref[...] = v_t.astype(BF16)
        nblk = k.shape[0] // MOBA_BLOCK
        km_ref[0] = jnp.mean(k.reshape(nblk, MOBA_BLOCK, a), axis=1)
    else:
        k_ref[...] = k
        v_ref[...] = v


def _in_projection(x2d, norm_g, w_bf16, b_in, tm, prompt, pos0, pos_stride, seq_len):
    t = x2d.shape[0]
    n_t = t // tm
    kern = functools.partial(_inproj_kernel, pos0=pos0, pos_stride=pos_stride, n_pos_tiles=seq_len // tm)
    row = lambda i: (i, 0)
    fixed = lambda i: (0, 0)
    n_pos = seq_len // tm
    wide = jax.ShapeDtypeStruct((t, ATTN_WIDTH), F32)
    wide_spec = pl.BlockSpec((tm, ATTN_WIDTH), row)
    if prompt:
        nblk = tm // MOBA_BLOCK
        n_seq = t // seq_len
        feat_spec = pl.BlockSpec((ATTN_WIDTH, tm), lambda i: (i // n_pos, i % n_pos))
        feat_f32 = jax.ShapeDtypeStruct((n_seq * ATTN_WIDTH, seq_len), F32)
        feat_bf16 = jax.ShapeDtypeStruct((n_seq * ATTN_WIDTH, seq_len), BF16)
        out_shape = [wide, feat_f32, feat_f32, wide, jax.ShapeDtypeStruct((t, ATTN_WIDTH), BF16), feat_bf16,
                     jax.ShapeDtypeStruct((n_t, nblk, ATTN_WIDTH), F32)]
        out_specs = [wide_spec, feat_spec, feat_spec, wide_spec, wide_spec, feat_spec,
                     pl.BlockSpec((1, nblk, ATTN_WIDTH), lambda i: (i, 0, 0))]
    else:
        out_shape = [wide] * 4
        out_specs = [wide_spec] * 4
    return pl.pallas_call(
        kern,
        grid=(n_t,),
        in_specs=[
            pl.BlockSpec((tm, D_MODEL), row),
            pl.BlockSpec((1, D_MODEL), fixed),
            pl.BlockSpec((D_MODEL, IN_WIDTH), fixed),
            pl.BlockSpec((1, IN_WIDTH), fixed),
        ],
        out_specs=out_specs,
        out_shape=out_shape,
        compiler_params=_cparams("parallel"),
        name="in_projection_prompt" if prompt else "in_projection_sample",
    )(x2d, norm_g, w_bf16, b_in)


def _moba_prompt_kernel(q_ref, kb_ref, vt_ref, km_ref, o_ref, m_sc, l_sc, acc_sc, bias_sc, s_even, s_odd):
    qi = pl.program_id(2)
    blk_rows = MOBA_BLOCK
    q = q_ref[...]
    lane = lax.broadcasted_iota(jnp.int32, (1, LANES), 1)
    head0 = lane < HEAD_DIM
    q2 = jnp.concatenate([jnp.where(head0, q, 0.0), jnp.where(head0, 0.0, q)], axis=0)
    km = km_ref[0]
    n_blocks = km.shape[0]
    nt = (((1,), (1,)), ((), ()))
    gate = lax.dot_general(km, q2, nt, precision=HIGHEST, preferred_element_type=F32)
    blk = lax.broadcasted_iota(jnp.int32, gate.shape, 0)
    fully_past = blk < qi
    gate = jnp.where(fully_past, gate, NEG_INF)
    rank = jnp.zeros(gate.shape, jnp.int32)
    for m in range(n_blocks):
        gm = gate[m:m + 1, :]
        ahead = (gm > gate) | ((gm == gate) & (m < blk))
        rank = rank + ahead.astype(jnp.int32)
    bias_sc[...] = jnp.where(fully_past & (rank < MOBA_TOPK), 0.0, NEG_INF)

    qb = (q2 * (SM_SCALE * LOG2_E)).astype(BF16)

    own0 = pl.multiple_of(qi * blk_rows, blk_rows)
    s = lax.dot_general(kb_ref[pl.ds(own0, blk_rows), :], qb, nt, preferred_element_type=F32)
    k_ix = lax.broadcasted_iota(jnp.int32, s.shape, 0)
    q_ix = lax.broadcasted_iota(jnp.int32, s.shape, 1) % blk_rows
    s = jnp.where(k_ix <= q_ix, s, NEG_INF)
    m0 = jnp.max(s, axis=0, keepdims=True)
    p = jnp.exp2(s - m0)
    m_sc[...] = m0
    l_sc[...] = jnp.sum(p, axis=0, keepdims=True)
    acc_sc[...] = jnp.dot(vt_ref[:, pl.ds(own0, blk_rows)], p.astype(BF16), preferred_element_type=F32)

    def scores(j0, n):
        r0 = pl.multiple_of(j0 * blk_rows, blk_rows)
        s = lax.dot_general(kb_ref[pl.ds(r0, n * blk_rows), :], qb, nt, preferred_element_type=F32)
        return jnp.concatenate([s[i * blk_rows:(i + 1) * blk_rows] + bias_sc[pl.ds(j0 + i, 1), :]
                                for i in range(n)], axis=0)

    def fold(s, j0, n):
        r0 = pl.multiple_of(j0 * blk_rows, blk_rows)
        m_old = m_sc[...]
        m_new = jnp.maximum(m_old, jnp.max(s, axis=0, keepdims=True))
        alpha = jnp.exp2(m_old - m_new)
        p = jnp.exp2(s - m_new)
        l_sc[...] = alpha * l_sc[...] + jnp.sum(p, axis=0, keepdims=True)
        acc_sc[...] = alpha * acc_sc[...] + jnp.dot(vt_ref[:, pl.ds(r0, n * blk_rows)], p.astype(BF16),
                                                    preferred_element_type=F32)
        m_sc[...] = m_new

    def past_blocks(j0, n):
        fold(scores(j0, n), j0, n)

    n_spans = qi // PAST_SPAN
    last_span = jnp.maximum(n_spans - 1, 0)
    span0 = lambda i: jnp.minimum(i, last_span) * PAST_SPAN

    @pl.when(n_spans > 0)
    def _():
        s_even[...] = scores(0, PAST_SPAN)

    def pair_body(t, carry):
        i = 2 * t
        s_odd[...] = scores(span0(i + 1), PAST_SPAN)
        fold(s_even[...], i * PAST_SPAN, PAST_SPAN)
        s_even[...] = scores(span0(i + 2), PAST_SPAN)
        fold(s_odd[...], (i + 1) * PAST_SPAN, PAST_SPAN)
        return carry

    lax.fori_loop(0, n_spans // 2, pair_body, 0)

    @pl.when(n_spans % 2 == 1)
    def _():
        fold(s_even[...], (n_spans - 1) * PAST_SPAN, PAST_SPAN)

    done = n_spans * PAST_SPAN
    n = PAST_SPAN // 2
    while n >= 1:
        @pl.when((qi & n) != 0)
        def _(done=done, n=n):
            past_blocks(done, n)
        done = done + (qi & n)
        n //= 2
    out = acc_sc[...] / l_sc[...]
    out_t = jnp.concatenate([out[:HEAD_DIM, :blk_rows], out[HEAD_DIM:, blk_rows:]], axis=0)
    o_ref[...] = out_t.T.astype(o_ref.dtype)


def _moba_prompt(q, kb, vtb, kmean, bsz, slen):
    n_q = slen // MOBA_BLOCK
    n_hp = ATTN_WIDTH // LANES
    cols2 = HEADS_PER_LANE_TILE * MOBA_BLOCK
    return pl.pallas_call(
        _moba_prompt_kernel,
        grid=(bsz, n_hp, n_q),
        in_specs=[
            pl.BlockSpec((MOBA_BLOCK, LANES), lambda b, hp, qi: (b * n_q + qi, hp)),
            pl.BlockSpec((slen, LANES), lambda b, hp, qi: (b, hp)),
            pl.BlockSpec((LANES, slen), lambda b, hp, qi: (b * n_hp + hp, 0)),
            pl.BlockSpec((1, n_q, LANES), lambda b, hp, qi: (b, 0, hp)),
        ],
        out_specs=pl.BlockSpec((MOBA_BLOCK, LANES), lambda b, hp, qi: (b * n_q + qi, hp)),
        out_shape=jax.ShapeDtypeStruct((bsz * slen, ATTN_WIDTH), BF16),
        scratch_shapes=[
            pltpu.VMEM((1, cols2), F32),
            pltpu.VMEM((1, cols2), F32),
            pltpu.VMEM((LANES, cols2), F32),
            pltpu.VMEM((n_q, cols2), F32),
            pltpu.VMEM((PAST_SPAN * MOBA_BLOCK, cols2), F32),
            pltpu.VMEM((PAST_SPAN * MOBA_BLOCK, cols2), F32),
        ],
        compiler_params=_cparams("parallel", "parallel", "arbitrary"),
        name="moba_prompt",
    )(q, kb, vtb, kmean)


def _ln_swish(c, g, b):
    mu = jnp.mean(c, axis=-1, keepdims=True)
    d = c - mu
    var = jnp.mean(d * d, axis=-1, keepdims=True)
    y = d * lax.rsqrt(var + LN_EPS) * g + b
    return y * jax.nn.sigmoid(y)


def _conv_prompt_kernel(prev_ref, cur_ref, w_ref, cb_ref, g_ref, b_ref, o_ref, ext_sc):
    i = pl.program_id(1)
    tc = cur_ref.shape[0]
    ext_sc[0:CONV_HALO, :] = jnp.where(i > 0, prev_ref[...], 0.0)
    ext_sc[CONV_HALO:CONV_HALO + tc, :] = cur_ref[...]
    ext_sc[CONV_HALO + tc:CONV_HALO + tc + SUBLANES, :] = jnp.zeros((SUBLANES, CONV_CH), F32)
    lead = CONV_HALO - (CONV_LEN - 1)
    for r0 in range(0, tc, CONV_CHUNK):
        acc = jnp.zeros((CONV_CHUNK, CONV_CH), F32)
        for res in range(SUBLANES):
            taps = [j for j in range(CONV_LEN) if (j + lead) % SUBLANES == res]
            part = None
            for j in taps:
                base = r0 + j + lead - res
                term = w_ref[j:j + 1, :] * ext_sc[base:base + CONV_CHUNK + SUBLANES, :]
                part = term if part is None else part + term
            if part is not None:
                acc = acc + part[res:res + CONV_CHUNK]
        y = _ln_swish(acc + cb_ref[...], g_ref[...], b_ref[...])
        o_ref[r0:r0 + CONV_CHUNK, :] = y.astype(o_ref.dtype)


def _conv_prompt(u2d, conv_w, conv_b, ln_g, ln_b, bsz, slen, tc):
    n_c = slen // tc
    halo_per_tile = tc // CONV_HALO
    fixed = lambda b, i: (0, 0)
    return pl.pallas_call(
        _conv_prompt_kernel,
        grid=(bsz, n_c),
        in_specs=[
            pl.BlockSpec((CONV_HALO, CONV_CH),
                         lambda b, i: (jnp.maximum((b * n_c + i) * halo_per_tile - 1, 0), 0)),
            pl.BlockSpec((tc, CONV_CH), lambda b, i: (b * n_c + i, 0)),
            pl.BlockSpec((CONV_LEN, CONV_CH), fixed),
            pl.BlockSpec((1, CONV_CH), fixed),
            pl.BlockSpec((1, CONV_CH), fixed),
            pl.BlockSpec((1, CONV_CH), fixed),
        ],
        out_specs=pl.BlockSpec((tc, CONV_CH), lambda b, i: (b * n_c + i, 0)),
        out_shape=jax.ShapeDtypeStruct((bsz * slen, CONV_CH), BF16),
        scratch_shapes=[pltpu.VMEM((CONV_HALO + tc + SUBLANES, CONV_CH), F32)],
        compiler_params=_cparams("parallel", "parallel"),
        name="conv_prompt",
    )(u2d, u2d, conv_w, conv_b, ln_g, ln_b)


def _conv_sample_kernel(st_ref, u_ref, w_ref, cb_ref, g_ref, b_ref, o_ref):
    hist = CONV_LEN - 1
    for b in range(st_ref.shape[0]):
        u_row = u_ref[b:b + 1, :]
        c = jnp.sum(st_ref[b] * w_ref[0:hist, :], axis=0, keepdims=True) + w_ref[hist:hist + 1, :] * u_row
        y = _ln_swish(c + cb_ref[...], g_ref[...], b_ref[...])
        o_ref[b:b + 1, :] = y.astype(o_ref.dtype)


def _conv_sample(state, u2d, conv_w, conv_b, ln_g, ln_b):
    n = u2d.shape[0]
    return pl.pallas_call(
        _conv_sample_kernel,
        out_shape=jax.ShapeDtypeStruct((n, CONV_CH), BF16),
        compiler_params=pltpu.CompilerParams(vmem_limit_bytes=VMEM_LIMIT),
        name="conv_sample",
    )(state, u2d, conv_w, conv_b, ln_g, ln_b)


def _route(logits, with_group=False):
    c = lax.broadcasted_iota(jnp.int32, logits.shape, 1)
    big = jnp.int32(ROUTER_COLS)
    is_group = (c >= GROUP_COL0) & (c < GROUP_COL0 + N_EXPERT_GROUPS)
    gl = jnp.where(is_group, logits, NEG_INF)
    gmax = jnp.max(gl, axis=1, keepdims=True)
    g_sel = jnp.min(jnp.where(gl == gmax, c - GROUP_COL0, big), axis=1, keepdims=True)
    g_w = 1.0 / jnp.sum(jnp.exp(gl - gmax), axis=1, keepdims=True)
    in_group = (c < N_EXPERTS) & ((c // EXPERTS_PER_GROUP) == g_sel)
    el = jnp.where(in_group, logits, NEG_INF)
    m1 = jnp.max(el, axis=1, keepdims=True)
    i1 = jnp.min(jnp.where(el == m1, c, big), axis=1, keepdims=True)
    el2 = jnp.where(c == i1, NEG_INF, el)
    m2 = jnp.max(el2, axis=1, keepdims=True)
    i2 = jnp.min(jnp.where(el2 == m2, c, big), axis=1, keepdims=True)
    e21 = jnp.exp(m2 - m1)
    inv = 1.0 / (1.0 + e21)
    comb = g_w * jnp.where(c == i1, inv, jnp.where(c == i2, e21 * inv, 0.0))
    return (comb, g_sel) if with_group else comb


def _post_kernel(x_ref, a_ref, c_ref, wo_ref, g2_ref, wr_ref, br_ref, wg_ref, wu_ref, wd_ref, gf_ref,
                 y_ref, x1_sc, h2_sc, comb_sc, acc_sc):
    e = pl.program_id(1)

    @pl.when(e == 0)
    def _():
        x1 = (x_ref[...]
              + jnp.dot(a_ref[...], wo_ref[0:ATTN_WIDTH, :], preferred_element_type=F32)
              + jnp.dot(c_ref[...], wo_ref[ATTN_WIDTH:D_MODEL, :], preferred_element_type=F32))
        x1_sc[...] = x1
        h2 = x1 * lax.rsqrt(jnp.mean(x1 * x1, axis=-1, keepdims=True) + RMS_EPS) * g2_ref[...]
        h2_sc[...] = h2.astype(BF16)
        logits = jnp.dot(h2, wr_ref[...], precision=HIGHEST, preferred_element_type=F32) + br_ref[...]
        comb_sc[...] = _route(logits)
        acc_sc[...] = jnp.zeros_like(acc_sc)

    h2b = h2_sc[...]
    col = lax.broadcasted_iota(jnp.int32, comb_sc.shape, 1)
    ce = jnp.sum(jnp.where(col == e, comb_sc[...], 0.0), axis=1, keepdims=True)
    gt = jnp.dot(h2b, wg_ref[0], preferred_element_type=F32)
    up = jnp.dot(h2b, wu_ref[0], preferred_element_type=F32)
    hid = gt * jax.nn.sigmoid(gt) * up * ce
    acc_sc[...] += jnp.dot(hid.astype(BF16), wd_ref[0], preferred_element_type=F32)

    @pl.when(e == N_EXPERTS - 1)
    def _():
        x2 = x1_sc[...] + acc_sc[...]
        y_ref[...] = x2 * lax.rsqrt(jnp.mean(x2 * x2, axis=-1, keepdims=True) + RMS_EPS) * gf_ref[...]


def _post_mixer(x2d, attn, conv, w_out, g2, w_rt, b_rt, wg, wu, wd, gf, tm, name):
    t = x2d.shape[0]
    row = lambda i, e: (i, 0)
    fixed = lambda i, e: (0, 0)
    return pl.pallas_call(
        _post_kernel,
        grid=(t // tm, N_EXPERTS),
        in_specs=[
            pl.BlockSpec((tm, D_MODEL), row),
            pl.BlockSpec((tm, ATTN_WIDTH), row),
            pl.BlockSpec((tm, CONV_CH), row),
            pl.BlockSpec((D_MODEL, D_MODEL), fixed),
            pl.BlockSpec((1, D_MODEL), fixed),
            pl.BlockSpec((D_MODEL, ROUTER_COLS), fixed),
            pl.BlockSpec((1, ROUTER_COLS), fixed),
            pl.BlockSpec((1, D_MODEL, D_EXPERT), lambda i, e: (e, 0, 0)),
            pl.BlockSpec((1, D_MODEL, D_EXPERT), lambda i, e: (e, 0, 0)),
            pl.BlockSpec((1, D_EXPERT, D_MODEL), lambda i, e: (e, 0, 0)),
            pl.BlockSpec((1, D_MODEL), fixed),
        ],
        out_specs=pl.BlockSpec((tm, D_MODEL), row),
        out_shape=jax.ShapeDtypeStruct((t, D_MODEL), F32),
        scratch_shapes=[
            pltpu.VMEM((tm, D_MODEL), F32),
            pltpu.VMEM((tm, D_MODEL), BF16),
            pltpu.VMEM((tm, ROUTER_COLS), F32),
            pltpu.VMEM((tm, D_MODEL), F32),
        ],
        compiler_params=_cparams("parallel", "arbitrary"),
        name=name,
    )(x2d, attn, conv, w_out, g2, w_rt, b_rt, wg, wu, wd, gf)


def _mix_route_kernel(x_ref, a_ref, c_ref, wo_ref, g2_ref, wr_ref, br_ref, x1_ref, row_ref):
    x1 = (x_ref[...]
          + jnp.dot(a_ref[...], wo_ref[0:ATTN_WIDTH, :], preferred_element_type=F32)
          + jnp.dot(c_ref[...], wo_ref[ATTN_WIDTH:D_MODEL, :], preferred_element_type=F32))
    x1_ref[...] = x1
    h2 = x1 * lax.rsqrt(jnp.mean(x1 * x1, axis=-1, keepdims=True) + RMS_EPS) * g2_ref[...]
    h_hi = h2.astype(BF16)
    h_lo = (h2 - h_hi.astype(F32)).astype(BF16)
    w = wr_ref[...]
    w_hi = w.astype(BF16)
    w_lo = (w - w_hi.astype(F32)).astype(BF16)
    logits = (jnp.dot(h_hi, w_hi, preferred_element_type=F32) + jnp.dot(h_lo, w_hi, preferred_element_type=F32)
              + jnp.dot(h_hi, w_lo, preferred_element_type=F32) + br_ref[...])
    comb, g_sel = _route(logits, with_group=True)
    col = lax.broadcasted_iota(jnp.int32, comb.shape, 1)
    row_ref[:, 0:D_MODEL] = h2
    row_ref[:, D_MODEL:ROW_WIDTH] = jnp.where(col == GROUP_ID_COL, g_sel.astype(F32), comb)


def _mix_route(x2d, attn, conv, w_out, g2, w_rt, b_rt, tm):
    t = x2d.shape[0]
    row = lambda i: (i, 0)
    fixed = lambda i: (0, 0)
    return pl.pallas_call(
        _mix_route_kernel,
        grid=(t // tm,),
        in_specs=[
            pl.BlockSpec((tm, D_MODEL), row),
            pl.BlockSpec((tm, ATTN_WIDTH), row),
            pl.BlockSpec((tm, CONV_CH), row),
            pl.BlockSpec((D_MODEL, D_MODEL), fixed),
            pl.BlockSpec((1, D_MODEL), fixed),
            pl.BlockSpec((D_MODEL, ROUTER_COLS), fixed),
            pl.BlockSpec((1, ROUTER_COLS), fixed),
        ],
        out_specs=[pl.BlockSpec((tm, D_MODEL), row), pl.BlockSpec((tm, ROW_WIDTH), row)],
        out_shape=[jax.ShapeDtypeStruct((t, D_MODEL), F32), jax.ShapeDtypeStruct((t, ROW_WIDTH), F32)],
        compiler_params=_cparams("parallel"),
        name="mix_route_prompt",
    )(x2d, attn, conv, w_out, g2, w_rt, b_rt)


def _gather_rows(idx_ref, base, src_hbm, dst, sem, n_rows):
    def body(c, carry):
        for k in range(GATHER_UNROLL):
            r = c * GATHER_UNROLL + k
            pltpu.make_async_copy(src_hbm.at[pl.ds(idx_ref[base + r], 1), :], dst.at[pl.ds(r, 1), :], sem).start()
        return carry
    lax.fori_loop(0, n_rows // GATHER_UNROLL, body, 0)


def _wait_rows(src_hbm, dst, sem, n_rows):
    def body(c, carry):
        for k in range(GATHER_UNROLL):
            r = c * GATHER_UNROLL + k
            pltpu.make_async_copy(src_hbm.at[pl.ds(0, 1), :], dst.at[pl.ds(r, 1), :], sem).wait()
        return carry
    lax.fori_loop(0, n_rows // GATHER_UNROLL, body, 0)


def _moe_grouped_kernel(tg_ref, src_ref, rows_hbm, wg_ref, wu_ref, wd_ref, y_ref, xbuf, sem, hb_sc, acc_sc,
                        *, n_tiles):
    i = pl.program_id(0)
    j = pl.program_id(1)
    slot = i % 2
    group = tg_ref[i]

    @pl.when(j == 0)
    def _():
        @pl.when(i == 0)
        def _():
            _gather_rows(src_ref, 0, rows_hbm, xbuf.at[0], sem.at[0], MOE_TILE)

        @pl.when(i + 1 < n_tiles)
        def _():
            _gather_rows(src_ref, (i + 1) * MOE_TILE, rows_hbm, xbuf.at[1 - slot], sem.at[1 - slot], MOE_TILE)

        _wait_rows(rows_hbm, xbuf.at[slot], sem.at[slot], MOE_TILE)
        hb_sc[...] = xbuf[slot, :, 0:D_MODEL].astype(BF16)
        acc_sc[...] = jnp.zeros_like(acc_sc)

    @pl.when(group < N_EXPERT_GROUPS)
    def _():
        e = group * EXPERTS_PER_GROUP + j
        comb = xbuf[slot, :, D_MODEL:ROW_WIDTH]
        col = lax.broadcasted_iota(jnp.int32, comb.shape, 1)
        ce = jnp.sum(jnp.where(col == e, comb, 0.0), axis=1, keepdims=True)
        hb = hb_sc[...]
        gt = jnp.dot(hb, wg_ref[0], preferred_element_type=F32)
        up = jnp.dot(hb, wu_ref[0], preferred_element_type=F32)
        hid = gt * jax.nn.sigmoid(gt) * up * ce
        acc_sc[...] += jnp.dot(hid.astype(BF16), wd_ref[0], preferred_element_type=F32)

    @pl.when(j == EXPERTS_PER_GROUP - 1)
    def _():
        y_ref[...] = acc_sc[...]


def _moe_grouped(tile_group, src_of_sorted, rows, wg, wu, wd, n_tiles):
    kern = functools.partial(_moe_grouped_kernel, n_tiles=n_tiles)
    last = N_EXPERT_GROUPS - 1
    w_ix = lambda i, j, tg, src: (jnp.minimum(tg[i], last) * EXPERTS_PER_GROUP + j, 0, 0)
    return pl.pallas_call(
        kern,
        grid_spec=pltpu.PrefetchScalarGridSpec(
            num_scalar_prefetch=2,
            grid=(n_tiles, EXPERTS_PER_GROUP),
            in_specs=[
                pl.BlockSpec(memory_space=pl.ANY),
                pl.BlockSpec((1, D_MODEL, D_EXPERT), w_ix),
                pl.BlockSpec((1, D_MODEL, D_EXPERT), w_ix),
                pl.BlockSpec((1, D_EXPERT, D_MODEL), w_ix),
            ],
            out_specs=pl.BlockSpec((MOE_TILE, D_MODEL), lambda i, j, tg, src: (i, 0)),
            scratch_shapes=[
                pltpu.VMEM((2, MOE_TILE, ROW_WIDTH), F32),
                pltpu.SemaphoreType.DMA((2,)),
                pltpu.VMEM((MOE_TILE, D_MODEL), BF16),
                pltpu.VMEM((MOE_TILE, D_MODEL), F32),
            ],
        ),
        out_shape=jax.ShapeDtypeStruct((n_tiles * MOE_TILE, D_MODEL), F32),
        compiler_params=_cparams("arbitrary", "arbitrary"),
        name="moe_grouped_prompt",
    )(tile_group, src_of_sorted, rows, wg, wu, wd)


def _moe_finish_kernel(pos_ref, x1_ref, gf_ref, ys_hbm, y_ref, buf, sem, *, n_tiles):
    i = pl.program_id(0)
    slot = i % 2
    tm = x1_ref.shape[0]

    @pl.when(i == 0)
    def _():
        _gather_rows(pos_ref, 0, ys_hbm, buf.at[0], sem.at[0], tm)

    @pl.when(i + 1 < n_tiles)
    def _():
        _gather_rows(pos_ref, (i + 1) * tm, ys_hbm, buf.at[1 - slot], sem.at[1 - slot], tm)

    _wait_rows(ys_hbm, buf.at[slot], sem.at[slot], tm)
    x2 = x1_ref[...] + buf[slot]
    y_ref[...] = x2 * lax.rsqrt(jnp.mean(x2 * x2, axis=-1, keepdims=True) + RMS_EPS) * gf_ref[...]


def _moe_finish(pos_of_token, x1, gf, ys, tm):
    t = x1.shape[0]
    n_tiles = t // tm
    kern = functools.partial(_moe_finish_kernel, n_tiles=n_tiles)
    return pl.pallas_call(
        kern,
        grid_spec=pltpu.PrefetchScalarGridSpec(
            num_scalar_prefetch=1,
            grid=(n_tiles,),
            in_specs=[
                pl.BlockSpec((tm, D_MODEL), lambda i, pos: (i, 0)),
                pl.BlockSpec((1, D_MODEL), lambda i, pos: (0, 0)),
                pl.BlockSpec(memory_space=pl.ANY),
            ],
            out_specs=pl.BlockSpec((tm, D_MODEL), lambda i, pos: (i, 0)),
            scratch_shapes=[pltpu.VMEM((2, tm, D_MODEL), F32), pltpu.SemaphoreType.DMA((2,))],
        ),
        out_shape=jax.ShapeDtypeStruct((t, D_MODEL), F32),
        compiler_params=_cparams("arbitrary"),
        name="moe_finish_prompt",
    )(pos_of_token, x1, gf, ys)


def _sort_plan(group_id, n_tiles):
    t = group_id.shape[0]
    onehot = (group_id[:, None] == jnp.arange(N_EXPERT_GROUPS, dtype=jnp.int32)[None, :]).astype(jnp.int32)
    counts = jnp.sum(onehot, axis=0)
    rank = jnp.sum((jnp.cumsum(onehot, axis=0) - onehot) * onehot, axis=1)
    padded = (counts + MOE_TILE - 1) // MOE_TILE * MOE_TILE
    group_end = jnp.cumsum(padded)
    group_start = group_end - padded
    pos = group_start[group_id] + rank
    src = jnp.zeros((n_tiles * MOE_TILE,), jnp.int32).at[pos].set(jnp.arange(t, dtype=jnp.int32))
    tile_start = jnp.arange(n_tiles, dtype=jnp.int32) * MOE_TILE
    tile_group = jnp.sum((tile_start[:, None] >= group_end[None, :]).astype(jnp.int32), axis=1)
    return pos.astype(jnp.int32), src, tile_group.astype(jnp.int32)


def _post_mixer_grouped(x2d, attn, conv, w_out_b, g2, w_rt, b_rt, wg_b, wu_b, wd_b, gf):
    t = x2d.shape[0]
    n_tiles = t // MOE_TILE + N_EXPERT_GROUPS
    x1, rows = _mix_route(x2d, attn, conv, w_out_b, g2, w_rt, b_rt, MOE_TILE)
    group_id = rows[:, D_MODEL + GROUP_ID_COL].astype(jnp.int32)
    pos, src, tile_group = _sort_plan(group_id, n_tiles)
    ys = _moe_grouped(tile_group, src, rows, wg_b, wu_b, wd_b, n_tiles)
    return _moe_finish(pos, x1, gf, ys, MOE_TILE)


def _kmean_select_kernel(pt_ref, q_ref, ck_hbm, sel_ref, buf, sem, sums_sc, *, n_pages, n_seq):
    b = pl.program_id(0)
    n_blocks = n_pages // PAGES_PER_BLOCK
    page_flat = ATTN_WIDTH

    def page_copy(seq, p, slot):
        start = pl.multiple_of(pt_ref[seq * n_pages + p] * page_flat, page_flat)
        return pltpu.make_async_copy(ck_hbm.at[pl.ds(start, page_flat), :], buf.at[slot], sem.at[slot])

    @pl.when(b == 0)
    def _():
        for s in range(KMEAN_RING):
            page_copy(0, s, s).start()

    sums_sc[...] = jnp.zeros_like(sums_sc)
    lane = lax.broadcasted_iota(jnp.int32, (1, LANES), 1)
    sub = SUBLANES
    groups = HEAD_DIM // sub

    def block_body(n, carry):
        part = jnp.zeros((N_HEADS, sub, LANES), F32)
        for pg in range(PAGES_PER_BLOCK):
            p = n * PAGES_PER_BLOCK + pg
            slot = p % KMEAN_RING
            page_copy(b, p, slot).wait()
            prod = buf[slot] * q_ref[0]
            part = part + jnp.sum(prod.reshape(N_HEADS, groups, sub, LANES), axis=1)
            ahead = p + KMEAN_RING

            @pl.when(ahead < n_pages)
            def _():
                page_copy(b, ahead, slot).start()

            @pl.when((ahead >= n_pages) & (b + 1 < n_seq))
            def _():
                page_copy(b + 1, ahead - n_pages, slot).start()
        block_sum = jnp.sum(part.reshape(N_HEADS * sub, LANES), axis=1, keepdims=True)
        sums_sc[...] = jnp.where(lane == n, block_sum, sums_sc[...])
        return carry

    lax.fori_loop(0, n_blocks, block_body, 0)

    fold = (lax.broadcasted_iota(jnp.int32, (N_HEADS, N_HEADS * sub), 1) // sub
            == lax.broadcasted_iota(jnp.int32, (N_HEADS, N_HEADS * sub), 0)).astype(F32)
    gate = jnp.dot(fold, sums_sc[...], precision=HIGHEST, preferred_element_type=F32) * (1.0 / MOBA_BLOCK)
    blk = lax.broadcasted_iota(jnp.int32, gate.shape, 1)
    gate = jnp.where(blk < n_blocks, gate, NEG_INF)
    rank = jnp.zeros(gate.shape, jnp.int32)
    for s in range(1, LANES):
        other = pltpu.roll(gate, s, axis=1)
        other_blk = jnp.where(blk >= s, blk - s, blk - s + LANES)
        ahead = (other > gate) | ((other == gate) & (other_blk < blk))
        rank = rank + ahead.astype(jnp.int32)
    out = jnp.zeros(gate.shape, jnp.int32)
    for r in range(MOBA_TOPK):
        picked = jnp.sum(jnp.where(rank == r, blk, 0).astype(F32), axis=1, keepdims=True)
        out = jnp.where(blk == r, picked.astype(jnp.int32), out)
    sel_ref[0] = out


def _kmean_select(page_table_flat, q_lanes, cache_kt, n_seq, n_pages):
    kern = functools.partial(_kmean_select_kernel, n_pages=n_pages, n_seq=n_seq)
    return pl.pallas_call(
        kern,
        grid_spec=pltpu.PrefetchScalarGridSpec(
            num_scalar_prefetch=1,
            grid=(n_seq,),
            in_specs=[
                pl.BlockSpec((1, ATTN_WIDTH, LANES), lambda b, pt: (b, 0, 0)),
                pl.BlockSpec(memory_space=pl.ANY),
            ],
            out_specs=pl.BlockSpec((1, N_HEADS, LANES), lambda b, pt: (b, 0, 0)),
            scratch_shapes=[
                pltpu.VMEM((KMEAN_RING, ATTN_WIDTH, LANES), F32),
                pltpu.SemaphoreType.DMA((KMEAN_RING,)),
                pltpu.VMEM((N_HEADS * SUBLANES, LANES), F32),
            ],
        ),
        out_shape=jax.ShapeDtypeStruct((n_seq, N_HEADS, LANES), jnp.int32),
        compiler_params=_cparams("arbitrary"),
        name="kmean_select_sample",
    )(page_table_flat, q_lanes, cache_kt)


def _moba_sample_kernel(pt_ref, sel_ref, q_ref, kn_ref, vn_ref, ck_hbm, cv_hbm, o_ref,
                        kbuf, vbuf, ksem, vsem, *, n_pages, n_seq):
    b = pl.program_id(0)
    half = b % 2

    def seq_copies(seq, dst_half):
        copies = []
        for h in range(N_HEADS):
            for r in range(MOBA_TOPK):
                blk = sel_ref[(seq * N_HEADS + h) * MOBA_TOPK + r]
                for pg in range(PAGES_PER_BLOCK):
                    page = pt_ref[seq * n_pages + blk * PAGES_PER_BLOCK + pg]
                    src = pl.ds(pl.multiple_of(page * ATTN_WIDTH + h * HEAD_DIM, HEAD_DIM), HEAD_DIM)
                    dst = pl.ds((r * PAGES_PER_BLOCK + pg) * LANES, LANES)
                    copies.append(pltpu.make_async_copy(ck_hbm.at[src, :], kbuf.at[dst_half, h, :, dst],
                                                        ksem.at[dst_half]))
                    copies.append(pltpu.make_async_copy(cv_hbm.at[src, :], vbuf.at[dst_half, h, :, dst],
                                                        vsem.at[dst_half]))
        return copies

    @pl.when(b == 0)
    def _():
        for cp in seq_copies(0, 0):
            cp.start()

    @pl.when(b + 1 < n_seq)
    def _():
        for cp in seq_copies(b + 1, 1 - half):
            cp.start()

    for cp in seq_copies(b, half):
        cp.wait()

    q8 = q_ref[0]
    kn = kn_ref[0]
    vn = vn_ref[0]
    nt = (((1,), (1,)), ((), ()))
    l_past = jnp.concatenate(
        [jnp.dot(q8, kbuf[half, h], precision=HIGHEST, preferred_element_type=F32)[h:h + 1, :]
         for h in range(N_HEADS)], axis=0) * SM_SCALE
    l_self = jnp.sum(q8 * kn, axis=1, keepdims=True) * SM_SCALE
    m = jnp.maximum(jnp.max(l_past, axis=1, keepdims=True), l_self)
    p = jnp.exp(l_past - m)
    p_self = jnp.exp(l_self - m)
    den = jnp.sum(p, axis=1, keepdims=True) + p_self
    pv = jnp.concatenate(
        [lax.dot_general(p, vbuf[half, h], nt, precision=HIGHEST, preferred_element_type=F32)[h:h + 1, :]
         for h in range(N_HEADS)], axis=0)
    o_ref[0] = (p_self * vn + pv) / den


def _moba_sample(page_table_flat, sel_flat, q3, k3, v3, cache_kt, cache_vt, n_seq, n_pages):
    kern = functools.partial(_moba_sample_kernel, n_pages=n_pages, n_seq=n_seq)
    n_keys = MOBA_TOPK * MOBA_BLOCK
    tok = pl.BlockSpec((1, N_HEADS, HEAD_DIM), lambda b, pt, sel: (b, 0, 0))
    return pl.pallas_call(
        kern,
        grid_spec=pltpu.PrefetchScalarGridSpec(
            num_scalar_prefetch=2,
            grid=(n_seq,),
            in_specs=[tok, tok, tok, pl.BlockSpec(memory_space=pl.ANY), pl.BlockSpec(memory_space=pl.ANY)],
            out_specs=tok,
            scratch_shapes=[
                pltpu.VMEM((2, N_HEADS, HEAD_DIM, n_keys), F32),
                pltpu.VMEM((2, N_HEADS, HEAD_DIM, n_keys), F32),
                pltpu.SemaphoreType.DMA((2,)),
                pltpu.SemaphoreType.DMA((2,)),
            ],
        ),
        out_shape=jax.ShapeDtypeStruct((n_seq, N_HEADS, HEAD_DIM), F32),
        compiler_params=_cparams("arbitrary"),
        name="moba_sample",
    )(page_table_flat, sel_flat, q3, k3, v3, cache_kt, cache_vt)


def _token_tile(t, want):
    return want if t % want == 0 else t


def kernel(x_prompt, x_sample, cache_k, cache_v, state_conv, page_table, norm1_g, w_in, b_in, conv_w, conv_b,
           conv_ln_g, conv_ln_b, w_out, norm2_g, w_group, b_group, w_router, b_router, w_gate, w_up, w_down,
           norm_f_g):
    bsz, slen, _ = x_prompt.shape
    dbsz, dlen, _ = x_sample.shape
    depth = w_in.shape[0]
    assert depth == 1 and dlen == 1, "one layer and one new sample token per sequence"
    n_pages, page_rows = page_table.shape[1], cache_k.shape[2]
    assert MOBA_BLOCK == PAGES_PER_BLOCK * page_rows and slen % MOBA_BLOCK == 0
    assert page_rows == LANES and MOBA_TOPK <= n_pages // PAGES_PER_BLOCK <= LANES and n_pages % KMEAN_RING == 0
    past_len = n_pages * page_rows
    l = 0

    w_in_b = w_in[l].astype(BF16)
    w_out_b = w_out[l].astype(BF16)
    wg_b, wu_b, wd_b = w_gate[l].astype(BF16), w_up[l].astype(BF16), w_down[l].astype(BF16)
    pad_cols = ROUTER_COLS - N_EXPERTS - N_EXPERT_GROUPS
    w_rt = jnp.pad(jnp.concatenate([w_router[l], w_group[l]], axis=1), ((0, 0), (0, pad_cols)))
    b_rt = jnp.pad(jnp.concatenate([b_router[l], b_group[l]])[None, :], ((0, 0), (0, pad_cols)))
    g1, g2, gf = norm1_g[l][None, :], norm2_g[l][None, :], norm_f_g[None, :]
    b_in2, cb = b_in[l][None, :], conv_b[l][None, :]
    lg, lb = conv_ln_g[l][None, :], conv_ln_b[l][None, :]

    t_p = bsz * slen
    xp = x_prompt.reshape(t_p, D_MODEL)
    tm_in = _token_tile(slen, 512)
    q_p, kt_p, vt_p, u_p, kb_p, vtb_p, km_p = _in_projection(xp, g1, w_in_b, b_in2, tm_in, True, 0, 1, slen)
    km_p = km_p.reshape(bsz, slen // MOBA_BLOCK, ATTN_WIDTH)
    attn_p = _moba_prompt(q_p, kb_p, vtb_p, km_p, bsz, slen)
    conv_p = _conv_prompt(u_p, conv_w[l], cb, lg, lb, bsz, slen, _token_tile(slen, 256))
    y_p = _post_mixer_grouped(xp, attn_p, conv_p, w_out_b, g2, w_rt, b_rt, wg_b, wu_b, wd_b, gf)

    xs = x_sample.reshape(dbsz, D_MODEL)
    q_s, k_s, v_s, u_s = _in_projection(xs, g1, w_in_b, b_in2, dbsz, False, past_len, 0, dbsz)
    heads = lambda t: t.reshape(dbsz, N_HEADS, HEAD_DIM)
    ck2 = cache_k[l].transpose(0, 2, 3, 1).reshape(-1, page_rows)
    cv2 = cache_v[l].transpose(0, 2, 3, 1).reshape(-1, page_rows)
    pt_flat = page_table.reshape(-1)
    q_lanes = jnp.broadcast_to(q_s[:, :, None], (dbsz, ATTN_WIDTH, LANES))
    sel = _kmean_select(pt_flat, q_lanes, ck2, dbsz, n_pages)
    sel_flat = sel[:, :, :MOBA_TOPK].reshape(-1)
    attn_s = _moba_sample(pt_flat, sel_flat, heads(q_s), heads(k_s), heads(v_s), ck2, cv2, dbsz, n_pages)
    conv_s = _conv_sample(state_conv[l], u_s, conv_w[l], cb, lg, lb)
    y_s = _post_mixer(xs, attn_s.reshape(dbsz, ATTN_WIDTH).astype(BF16), conv_s, w_out_b, g2, w_rt, b_rt,
                      wg_b, wu_b, wd_b, gf, dbsz, "post_mixer_sample")

    hist = CONV_LEN - 1
    kv_p = lambda t: t.reshape(bsz, N_HEADS, HEAD_DIM, slen).transpose(0, 3, 1, 2)[None]
    kv_s = lambda t: t.reshape(1, dbsz, 1, N_HEADS, HEAD_DIM)
    new_conv_p = u_p.reshape(bsz, slen, CONV_CH)[:, slen - hist:][None]
    new_conv_s = jnp.concatenate([state_conv[l][:, 1:], u_s[:, None, :]], axis=1)[None]
    return (y_p.reshape(bsz, slen, D_MODEL), y_s.reshape(dbsz, 1, D_MODEL), kv_p(kt_p), kv_p(vt_p), new_conv_p,
            kv_s(k_s), kv_s(v_s), new_conv_s)
```

```python
import functools
import math

import jax
import jax.numpy as jnp
import numpy as np
from jax import lax
from jax.experimental import pallas as pl
from jax.experimental.pallas import tpu as pltpu

F32 = jnp.float32
BF16 = jnp.bfloat16
HIGHEST = lax.Precision.HIGHEST

D_MODEL = 1024
ATTN_WIDTH = 512
CONV_CH = 512
HEAD_DIM = 64
N_HEADS = 8
IN_WIDTH = 3 * ATTN_WIDTH + 2 * CONV_CH
CONV_LEN = 31
MOBA_BLOCK = 256
MOBA_TOPK = 3
ROPE_THETA = 10000.0
N_EXPERT_GROUPS = 4
EXPERTS_PER_GROUP = 4
N_EXPERTS = 16
D_EXPERT = 512
RMS_EPS = 1e-6
LN_EPS = 1e-5
NEG_INF = -1e30
SM_SCALE = HEAD_DIM ** -0.5
_ROPE_LOG_STEP = -math.log(ROPE_THETA) / (HEAD_DIM // 2)
ROPE_LOG_STEP_HI = float(np.float32(_ROPE_LOG_STEP))
ROPE_LOG_STEP_LO = _ROPE_LOG_STEP - ROPE_LOG_STEP_HI

LANES = 128
SUBLANES = 8
HEADS_PER_LANE_TILE = LANES // HEAD_DIM
ROUTER_COLS = LANES
GROUP_COL0 = N_EXPERTS
VMEM_LIMIT = 56 * 1024 * 1024
CONV_HALO = 32
CONV_CHUNK = 64
LOG2_E = math.log2(math.e)
PAST_SPAN = 2
KMEAN_RING = 16
PAGES_PER_BLOCK = 2
MOE_TILE = 1024
TOKEN_TILE = 512
GATHER_UNROLL = 8
ROW_WIDTH = D_MODEL + ROUTER_COLS
GROUP_ID_COL = ROUTER_COLS - 1


def _cparams(*sem):
    return pltpu.CompilerParams(dimension_semantics=sem, vmem_limit_bytes=VMEM_LIMIT)


def _inproj_kernel(x_ref, g_ref, w_ref, b_ref, q_ref, k_ref, v_ref, u_ref, *prompt_refs,
                   pos0, pos_stride, n_pos_tiles):
    x = x_ref[...]
    tm = x.shape[0]
    h = x * lax.rsqrt(jnp.mean(x * x, axis=-1, keepdims=True) + RMS_EPS) * g_ref[...]
    z = jnp.dot(h.astype(BF16), w_ref[...], preferred_element_type=F32) + b_ref[...]
    a = ATTN_WIDTH
    reps = a // LANES
    half = HEAD_DIM // 2
    lane_t = lax.broadcasted_iota(jnp.int32, (1, LANES), 1)
    freq_ix = (lane_t % half).astype(F32)
    inv_freq = jnp.exp(freq_ix * ROPE_LOG_STEP_HI) * jnp.exp(freq_ix * ROPE_LOG_STEP_LO)
    tile_pos = pos0 + (pl.program_id(0) % n_pos_tiles) * (tm * pos_stride)
    pos = tile_pos + lax.broadcasted_iota(jnp.int32, (tm, 1), 0) * pos_stride
    ang = pos.astype(F32) * inv_freq
    sin_sign = jnp.where((lane_t % HEAD_DIM) < half, -1.0, 1.0)
    cos = jnp.concatenate([jnp.cos(ang)] * reps, axis=-1)
    sin = jnp.concatenate([jnp.sin(ang) * sin_sign] * reps, axis=-1)
    lane = lax.broadcasted_iota(jnp.int32, (1, a), 1)
    first_half = (lane % HEAD_DIM) < half

    def rope(t):
        from_below = pltpu.roll(t, HEAD_DIM // 2, axis=1)
        from_above = pltpu.roll(t, a - HEAD_DIM // 2, axis=1)
        return t * cos + jnp.where(first_half, from_above, from_below) * sin

    q = rope(z[:, 0:a])
    k = rope(z[:, a:2 * a])
    v = z[:, 2 * a:3 * a]
    u = z[:, 3 * a:3 * a + CONV_CH] * jax.nn.sigmoid(z[:, 3 * a + CONV_CH:])
    q_ref[...] = q
    u_ref[...] = u
    if prompt_refs:
        kb_ref, vtb_ref, km_ref = prompt_refs
        v_t = v.T
        k_ref[...] = k.T
        v_ref[...] = v_t
        kb_ref[...] = k.astype(BF16)
        vtb_ref[...] = v_t.astype(BF16)
        nblk = k.shape[0] // MOBA_BLOCK
        km_ref[0] = jnp.mean(k.reshape(nblk, MOBA_BLOCK, a), axis=1)
    else:
        k_ref[...] = k
        v_ref[...] = v


def _in_projection(x2d, norm_g, w_bf16, b_in, tm, prompt, pos0, pos_stride, seq_len):
    t = x2d.shape[0]
    n_t = t // tm
    kern = functools.partial(_inproj_kernel, pos0=pos0, pos_stride=pos_stride, n_pos_tiles=seq_len // tm)
    row = lambda i: (i, 0)
    fixed = lambda i: (0, 0)
    n_pos = seq_len // tm
    wide = jax.ShapeDtypeStruct((t, ATTN_WIDTH), F32)
    wide_spec = pl.BlockSpec((tm, ATTN_WIDTH), row)
    if prompt:
        nblk = tm // MOBA_BLOCK
        n_seq = t // seq_len
        feat_spec = pl.BlockSpec((ATTN_WIDTH, tm), lambda i: (i // n_pos, i % n_pos))
        feat_f32 = jax.ShapeDtypeStruct((n_seq * ATTN_WIDTH, seq_len), F32)
        feat_bf16 = jax.ShapeDtypeStruct((n_seq * ATTN_WIDTH, seq_len), BF16)
        out_shape = [wide, feat_f32, feat_f32, wide, jax.ShapeDtypeStruct((t, ATTN_WIDTH), BF16), feat_bf16,
                     jax.ShapeDtypeStruct((n_t, nblk, ATTN_WIDTH), F32)]
        out_specs = [wide_spec, feat_spec, feat_spec, wide_spec, wide_spec, feat_spec,
                     pl.BlockSpec((1, nblk, ATTN_WIDTH), lambda i: (i, 0, 0))]
    else:
        out_shape = [wide] * 4
        out_specs = [wide_spec] * 4
    return pl.pallas_call(
        kern,
        grid=(n_t,),
        in_specs=[
            pl.BlockSpec((tm, D_MODEL), row),
            pl.BlockSpec((1, D_MODEL), fixed),
            pl.BlockSpec((D_MODEL, IN_WIDTH), fixed),
            pl.BlockSpec((1, IN_WIDTH), fixed),
        ],
        out_specs=out_specs,
        out_shape=out_shape,
        compiler_params=_cparams("parallel"),
        name="in_projection_prompt" if prompt else "in_projection_sample",
    )(x2d, norm_g, w_bf16, b_in)


def _moba_prompt_kernel(q_ref, kb_ref, vt_ref, km_ref, o_ref, m_sc, l_sc, acc_sc, bias_sc, s_even, s_odd):
    qi = pl.program_id(2)
    blk_rows = MOBA_BLOCK
    q = q_ref[...]
    lane = lax.broadcasted_iota(jnp.int32, (1, LANES), 1)
    head0 = lane < HEAD_DIM
    q2 = jnp.concatenate([jnp.where(head0, q, 0.0), jnp.where(head0, 0.0, q)], axis=0)
    km = km_ref[0]
    n_blocks = km.shape[0]
    nt = (((1,), (1,)), ((), ()))
    km_hi, q_hi = km.astype(BF16), q2.astype(BF16)
    km_lo, q_lo = (km - km_hi.astype(F32)).astype(BF16), (q2 - q_hi.astype(F32)).astype(BF16)
    gate = (lax.dot_general(km_hi, q_hi, nt, preferred_element_type=F32)
            + lax.dot_general(km_lo, q_hi, nt, preferred_element_type=F32)
            + lax.dot_general(km_hi, q_lo, nt, preferred_element_type=F32))
    blk = lax.broadcasted_iota(jnp.int32, gate.shape, 0)
    fully_past = blk < qi
    gate = jnp.where(fully_past, gate, NEG_INF)
    rank = jnp.zeros(gate.shape, jnp.int32)
    for m in range(n_blocks):
        gm = gate[m:m + 1, :]
        ahead = (gm > gate) | ((gm == gate) & (m < blk))
        rank = rank + ahead.astype(jnp.int32)
    bias_sc[...] = jnp.where(fully_past & (rank < MOBA_TOPK), 0.0, NEG_INF)

    qb = (q2 * (SM_SCALE * LOG2_E)).astype(BF16)

    def scores(j0):
        r0 = pl.multiple_of(j0 * blk_rows, blk_rows)
        s = lax.dot_general(kb_ref[pl.ds(r0, PAST_SPAN * blk_rows), :], qb, nt, preferred_element_type=F32)
        return jnp.concatenate([s[i * blk_rows:(i + 1) * blk_rows] + bias_sc[pl.ds(j0 + i, 1), :]
                                for i in range(PAST_SPAN)], axis=0)

    def fold(s, j0):
        r0 = pl.multiple_of(j0 * blk_rows, blk_rows)
        m_old = m_sc[...]
        m_new = jnp.maximum(m_old, jnp.max(s, axis=0, keepdims=True))
        alpha = jnp.exp2(m_old - m_new)
        p = jnp.exp2(s - m_new)
        l_sc[...] = alpha * l_sc[...] + jnp.sum(p, axis=0, keepdims=True)
        acc_sc[...] = alpha * acc_sc[...] + jnp.dot(vt_ref[:, pl.ds(r0, PAST_SPAN * blk_rows)], p.astype(BF16),
                                                    preferred_element_type=F32)
        m_sc[...] = m_new

    n_spans = (qi + PAST_SPAN - 1) // PAST_SPAN
    last_span = jnp.maximum(n_spans - 1, 0)
    span0 = lambda i: jnp.minimum(i, last_span) * PAST_SPAN
    s_even[...] = scores(0)

    own0 = pl.multiple_of(qi * blk_rows, blk_rows)
    s = lax.dot_general(kb_ref[pl.ds(own0, blk_rows), :], qb, nt, preferred_element_type=F32)
    k_ix = lax.broadcasted_iota(jnp.int32, s.shape, 0)
    q_ix = lax.broadcasted_iota(jnp.int32, s.shape, 1) % blk_rows
    s = jnp.where(k_ix <= q_ix, s, NEG_INF)
    m0 = jnp.max(s, axis=0, keepdims=True)
    p = jnp.exp2(s - m0)
    m_sc[...] = m0
    l_sc[...] = jnp.sum(p, axis=0, keepdims=True)
    acc_sc[...] = jnp.dot(vt_ref[:, pl.ds(own0, blk_rows)], p.astype(BF16), preferred_element_type=F32)

    def pair_body(t, carry):
        i = 2 * t
        s_odd[...] = scores(span0(i + 1))
        fold(s_even[...], i * PAST_SPAN)
        s_even[...] = scores(span0(i + 2))
        fold(s_odd[...], (i + 1) * PAST_SPAN)
        return carry

    lax.fori_loop(0, n_spans // 2, pair_body, 0)

    @pl.when(n_spans % 2 == 1)
    def _():
        fold(s_even[...], (n_spans - 1) * PAST_SPAN)

    out = acc_sc[...] / l_sc[...]
    out_t = jnp.concatenate([out[:HEAD_DIM, :blk_rows], out[HEAD_DIM:, blk_rows:]], axis=0)
    o_ref[...] = out_t.T.astype(o_ref.dtype)


def _moba_prompt(q, kb, vtb, kmean, bsz, slen):
    n_q = slen // MOBA_BLOCK
    n_hp = ATTN_WIDTH // LANES
    cols2 = HEADS_PER_LANE_TILE * MOBA_BLOCK
    return pl.pallas_call(
        _moba_prompt_kernel,
        grid=(bsz, n_hp, n_q),
        in_specs=[
            pl.BlockSpec((MOBA_BLOCK, LANES), lambda b, hp, qi: (b * n_q + qi, hp)),
            pl.BlockSpec((slen, LANES), lambda b, hp, qi: (b, hp)),
            pl.BlockSpec((LANES, slen), lambda b, hp, qi: (b * n_hp + hp, 0)),
            pl.BlockSpec((1, n_q, LANES), lambda b, hp, qi: (b, 0, hp)),
        ],
        out_specs=pl.BlockSpec((MOBA_BLOCK, LANES), lambda b, hp, qi: (b * n_q + qi, hp)),
        out_shape=jax.ShapeDtypeStruct((bsz * slen, ATTN_WIDTH), BF16),
        scratch_shapes=[
            pltpu.VMEM((1, cols2), F32),
            pltpu.VMEM((1, cols2), F32),
            pltpu.VMEM((LANES, cols2), F32),
            pltpu.VMEM((n_q, cols2), F32),
            pltpu.VMEM((PAST_SPAN * MOBA_BLOCK, cols2), F32),
            pltpu.VMEM((PAST_SPAN * MOBA_BLOCK, cols2), F32),
        ],
        compiler_params=_cparams("parallel", "parallel", "arbitrary"),
        name="moba_prompt",
    )(q, kb, vtb, kmean)


def _ln_swish(c, g, b):
    mu = jnp.mean(c, axis=-1, keepdims=True)
    d = c - mu
    var = jnp.mean(d * d, axis=-1, keepdims=True)
    y = d * lax.rsqrt(var + LN_EPS) * g + b
    return y * jax.nn.sigmoid(y)


def _conv_prompt_kernel(prev_ref, cur_ref, w_ref, cb_ref, g_ref, b_ref, o_ref, ext_sc):
    i = pl.program_id(1)
    tc = cur_ref.shape[0]
    ext_sc[0:CONV_HALO, :] = jnp.where(i > 0, prev_ref[...], 0.0)
    ext_sc[CONV_HALO:CONV_HALO + tc, :] = cur_ref[...]
    ext_sc[CONV_HALO + tc:CONV_HALO + tc + SUBLANES, :] = jnp.zeros((SUBLANES, CONV_CH), F32)
    lead = CONV_HALO - (CONV_LEN - 1)
    for r0 in range(0, tc, CONV_CHUNK):
        acc = jnp.zeros((CONV_CHUNK, CONV_CH), F32)
        for res in range(SUBLANES):
            taps = [j for j in range(CONV_LEN) if (j + lead) % SUBLANES == res]
            part = None
            for j in taps:
                base = r0 + j + lead - res
                term = w_ref[j:j + 1, :] * ext_sc[base:base + CONV_CHUNK + SUBLANES, :]
                part = term if part is None else part + term
            if part is not None:
                acc = acc + part[res:res + CONV_CHUNK]
        y = _ln_swish(acc + cb_ref[...], g_ref[...], b_ref[...])
        o_ref[r0:r0 + CONV_CHUNK, :] = y.astype(o_ref.dtype)


def _conv_prompt(u2d, conv_w, conv_b, ln_g, ln_b, bsz, slen, tc):
    n_c = slen // tc
    halo_per_tile = tc // CONV_HALO
    fixed = lambda b, i: (0, 0)
    return pl.pallas_call(
        _conv_prompt_kernel,
        grid=(bsz, n_c),
        in_specs=[
            pl.BlockSpec((CONV_HALO, CONV_CH),
                         lambda b, i: (jnp.maximum((b * n_c + i) * halo_per_tile - 1, 0), 0)),
            pl.BlockSpec((tc, CONV_CH), lambda b, i: (b * n_c + i, 0)),
            pl.BlockSpec((CONV_LEN, CONV_CH), fixed),
            pl.BlockSpec((1, CONV_CH), fixed),
            pl.BlockSpec((1, CONV_CH), fixed),
            pl.BlockSpec((1, CONV_CH), fixed),
        ],
        out_specs=pl.BlockSpec((tc, CONV_CH), lambda b, i: (b * n_c + i, 0)),
        out_shape=jax.ShapeDtypeStruct((bsz * slen, CONV_CH), BF16),
        scratch_shapes=[pltpu.VMEM((CONV_HALO + tc + SUBLANES, CONV_CH), F32)],
        compiler_params=_cparams("parallel", "parallel"),
        name="conv_prompt",
    )(u2d, u2d, conv_w, conv_b, ln_g, ln_b)


def _conv_sample_kernel(st_ref, u_ref, w_ref, cb_ref, g_ref, b_ref, o_ref):
    hist = CONV_LEN - 1
    for b in range(st_ref.shape[0]):
        u_row = u_ref[b:b + 1, :]
        c = jnp.sum(st_ref[b] * w_ref[0:hist, :], axis=0, keepdims=True) + w_ref[hist:hist + 1, :] * u_row
        y = _ln_swish(c + cb_ref[...], g_ref[...], b_ref[...])
        o_ref[b:b + 1, :] = y.astype(o_ref.dtype)


def _conv_sample(state, u2d, conv_w, conv_b, ln_g, ln_b):
    n = u2d.shape[0]
    return pl.pallas_call(
        _conv_sample_kernel,
        out_shape=jax.ShapeDtypeStruct((n, CONV_CH), BF16),
        compiler_params=pltpu.CompilerParams(vmem_limit_bytes=VMEM_LIMIT),
        name="conv_sample",
    )(state, u2d, conv_w, conv_b, ln_g, ln_b)


def _route(logits, with_group=False):
    c = lax.broadcasted_iota(jnp.int32, logits.shape, 1)
    big = jnp.int32(ROUTER_COLS)
    is_group = (c >= GROUP_COL0) & (c < GROUP_COL0 + N_EXPERT_GROUPS)
    gl = jnp.where(is_group, logits, NEG_INF)
    gmax = jnp.max(gl, axis=1, keepdims=True)
    g_sel = jnp.min(jnp.where(gl == gmax, c - GROUP_COL0, big), axis=1, keepdims=True)
    g_w = 1.0 / jnp.sum(jnp.exp(gl - gmax), axis=1, keepdims=True)
    in_group = (c < N_EXPERTS) & ((c // EXPERTS_PER_GROUP) == g_sel)
    el = jnp.where(in_group, logits, NEG_INF)
    m1 = jnp.max(el, axis=1, keepdims=True)
    i1 = jnp.min(jnp.where(el == m1, c, big), axis=1, keepdims=True)
    el2 = jnp.where(c == i1, NEG_INF, el)
    m2 = jnp.max(el2, axis=1, keepdims=True)
    i2 = jnp.min(jnp.where(el2 == m2, c, big), axis=1, keepdims=True)
    e21 = jnp.exp(m2 - m1)
    inv = 1.0 / (1.0 + e21)
    comb = g_w * jnp.where(c == i1, inv, jnp.where(c == i2, e21 * inv, 0.0))
    return (comb, g_sel) if with_group else comb


def _post_kernel(x_ref, a_ref, c_ref, wo_ref, g2_ref, wr_ref, br_ref, wg_ref, wu_ref, wd_ref, gf_ref,
                 y_ref, x1_sc, h2_sc, comb_sc, acc_sc):
    e = pl.program_id(1)

    @pl.when(e == 0)
    def _():
        x1 = (x_ref[...]
              + jnp.dot(a_ref[...], wo_ref[0:ATTN_WIDTH, :], preferred_element_type=F32)
              + jnp.dot(c_ref[...], wo_ref[ATTN_WIDTH:D_MODEL, :], preferred_element_type=F32))
        x1_sc[...] = x1
        h2 = x1 * lax.rsqrt(jnp.mean(x1 * x1, axis=-1, keepdims=True) + RMS_EPS) * g2_ref[...]
        h2_sc[...] = h2.astype(BF16)
        logits = jnp.dot(h2, wr_ref[...], precision=HIGHEST, preferred_element_type=F32) + br_ref[...]
        comb_sc[...] = _route(logits)
        acc_sc[...] = jnp.zeros_like(acc_sc)

    h2b = h2_sc[...]
    col = lax.broadcasted_iota(jnp.int32, comb_sc.shape, 1)
    ce = jnp.sum(jnp.where(col == e, comb_sc[...], 0.0), axis=1, keepdims=True)
    gt = jnp.dot(h2b, wg_ref[0], preferred_element_type=F32)
    up = jnp.dot(h2b, wu_ref[0], preferred_element_type=F32)
    hid = gt * jax.nn.sigmoid(gt) * up * ce
    acc_sc[...] += jnp.dot(hid.astype(BF16), wd_ref[0], preferred_element_type=F32)

    @pl.when(e == N_EXPERTS - 1)
    def _():
        x2 = x1_sc[...] + acc_sc[...]
        y_ref[...] = x2 * lax.rsqrt(jnp.mean(x2 * x2, axis=-1, keepdims=True) + RMS_EPS) * gf_ref[...]


def _post_mixer(x2d, attn, conv, w_out, g2, w_rt, b_rt, wg, wu, wd, gf, tm, name):
    t = x2d.shape[0]
    row = lambda i, e: (i, 0)
    fixed = lambda i, e: (0, 0)
    return pl.pallas_call(
        _post_kernel,
        grid=(t // tm, N_EXPERTS),
        in_specs=[
            pl.BlockSpec((tm, D_MODEL), row),
            pl.BlockSpec((tm, ATTN_WIDTH), row),
            pl.BlockSpec((tm, CONV_CH), row),
            pl.BlockSpec((D_MODEL, D_MODEL), fixed),
            pl.BlockSpec((1, D_MODEL), fixed),
            pl.BlockSpec((D_MODEL, ROUTER_COLS), fixed),
            pl.BlockSpec((1, ROUTER_COLS), fixed),
            pl.BlockSpec((1, D_MODEL, D_EXPERT), lambda i, e: (e, 0, 0)),
            pl.BlockSpec((1, D_MODEL, D_EXPERT), lambda i, e: (e, 0, 0)),
            pl.BlockSpec((1, D_EXPERT, D_MODEL), lambda i, e: (e, 0, 0)),
            pl.BlockSpec((1, D_MODEL), fixed),
        ],
        out_specs=pl.BlockSpec((tm, D_MODEL), row),
        out_shape=jax.ShapeDtypeStruct((t, D_MODEL), F32),
        scratch_shapes=[
            pltpu.VMEM((tm, D_MODEL), F32),
            pltpu.VMEM((tm, D_MODEL), BF16),
            pltpu.VMEM((tm, ROUTER_COLS), F32),
            pltpu.VMEM((tm, D_MODEL), F32),
        ],
        compiler_params=_cparams("parallel", "arbitrary"),
        name=name,
    )(x2d, attn, conv, w_out, g2, w_rt, b_rt, wg, wu, wd, gf)


def _mix_route_kernel(x_ref, a_ref, c_ref, wo_ref, g2_ref, wr_ref, br_ref, x1_ref, row_ref):
    x1 = (x_ref[...]
          + jnp.dot(a_ref[...], wo_ref[0:ATTN_WIDTH, :], preferred_element_type=F32)
          + jnp.dot(c_ref[...], wo_ref[ATTN_WIDTH:D_MODEL, :], preferred_element_type=F32))
    x1_ref[...] = x1
    h2 = x1 * lax.rsqrt(jnp.mean(x1 * x1, axis=-1, keepdims=True) + RMS_EPS) * g2_ref[...]
    h_hi = h2.astype(BF16)
    h_lo = (h2 - h_hi.astype(F32)).astype(BF16)
    w = wr_ref[...]
    w_hi = w.astype(BF16)
    w_lo = (w - w_hi.astype(F32)).astype(BF16)
    logits = (jnp.dot(h_hi, w_hi, preferred_element_type=F32) + jnp.dot(h_lo, w_hi, preferred_element_type=F32)
              + jnp.dot(h_hi, w_lo, preferred_element_type=F32) + br_ref[...])
    comb, g_sel = _route(logits, with_group=True)
    col = lax.broadcasted_iota(jnp.int32, comb.shape, 1)
    row_ref[:, 0:D_MODEL] = h2
    row_ref[:, D_MODEL:ROW_WIDTH] = jnp.where(col == GROUP_ID_COL, g_sel.astype(F32), comb)


def _mix_route(x2d, attn, conv, w_out, g2, w_rt, b_rt, tm):
    t = x2d.shape[0]
    row = lambda i: (i, 0)
    fixed = lambda i: (0, 0)
    return pl.pallas_call(
        _mix_route_kernel,
        grid=(t // tm,),
        in_specs=[
            pl.BlockSpec((tm, D_MODEL), row),
            pl.BlockSpec((tm, ATTN_WIDTH), row),
            pl.BlockSpec((tm, CONV_CH), row),
            pl.BlockSpec((D_MODEL, D_MODEL), fixed),
            pl.BlockSpec((1, D_MODEL), fixed),
            pl.BlockSpec((D_MODEL, ROUTER_COLS), fixed),
            pl.BlockSpec((1, ROUTER_COLS), fixed),
        ],
        out_specs=[pl.BlockSpec((tm, D_MODEL), row), pl.BlockSpec((tm, ROW_WIDTH), row)],
        out_shape=[jax.ShapeDtypeStruct((t, D_MODEL), F32), jax.ShapeDtypeStruct((t, ROW_WIDTH), F32)],
        compiler_params=_cparams("parallel"),
        name="mix_route_prompt",
    )(x2d, attn, conv, w_out, g2, w_rt, b_rt)


def _gather_rows(idx_ref, base, src_hbm, dst, sem, n_rows):
    def body(c, carry):
        for k in range(GATHER_UNROLL):
            r = c * GATHER_UNROLL + k
            pltpu.make_async_copy(src_hbm.at[pl.ds(idx_ref[base + r], 1), :], dst.at[pl.ds(r, 1), :], sem).start()
        return carry
    lax.fori_loop(0, n_rows // GATHER_UNROLL, body, 0)


def _wait_rows(src_hbm, dst, sem, n_rows):
    def body(c, carry):
        for k in range(GATHER_UNROLL):
            r = c * GATHER_UNROLL + k
            pltpu.make_async_copy(src_hbm.at[pl.ds(0, 1), :], dst.at[pl.ds(r, 1), :], sem).wait()
        return carry
    lax.fori_loop(0, n_rows // GATHER_UNROLL, body, 0)


def _moe_grouped_kernel(tg_ref, src_ref, rows_hbm, wg_ref, wu_ref, wd_ref, y_ref, xbuf, sem, hb_sc, acc_sc,
                        *, n_tiles):
    i = pl.program_id(0)
    j = pl.program_id(1)
    slot = i % 2
    group = tg_ref[i]

    @pl.when(j == 0)
    def _():
        @pl.when(i == 0)
        def _():
            _gather_rows(src_ref, 0, rows_hbm, xbuf.at[0], sem.at[0], MOE_TILE)

        @pl.when(i + 1 < n_tiles)
        def _():
            _gather_rows(src_ref, (i + 1) * MOE_TILE, rows_hbm, xbuf.at[1 - slot], sem.at[1 - slot], MOE_TILE)

        _wait_rows(rows_hbm, xbuf.at[slot], sem.at[slot], MOE_TILE)
        hb_sc[...] = xbuf[slot, :, 0:D_MODEL].astype(BF16)
        acc_sc[...] = jnp.zeros_like(acc_sc)

    @pl.when(group < N_EXPERT_GROUPS)
    def _():
        e = group * EXPERTS_PER_GROUP + j
        comb = xbuf[slot, :, D_MODEL:ROW_WIDTH]
        col = lax.broadcasted_iota(jnp.int32, comb.shape, 1)
        ce = jnp.sum(jnp.where(col == e, comb, 0.0), axis=1, keepdims=True)
        hb = hb_sc[...]
        gt = jnp.dot(hb, wg_ref[0], preferred_element_type=F32)
        up = jnp.dot(hb, wu_ref[0], preferred_element_type=F32)
        hid = gt * jax.nn.sigmoid(gt) * up * ce
        acc_sc[...] += jnp.dot(hid.astype(BF16), wd_ref[0], preferred_element_type=F32)

    @pl.when(j == EXPERTS_PER_GROUP - 1)
    def _():
        y_ref[...] = acc_sc[...]


def _moe_grouped(tile_group, src_of_sorted, rows, wg, wu, wd, n_tiles):
    kern = functools.partial(_moe_grouped_kernel, n_tiles=n_tiles)
    last = N_EXPERT_GROUPS - 1
    w_ix = lambda i, j, tg, src: (jnp.minimum(tg[i], last) * EXPERTS_PER_GROUP + j, 0, 0)
    return pl.pallas_call(
        kern,
        grid_spec=pltpu.PrefetchScalarGridSpec(
            num_scalar_prefetch=2,
            grid=(n_tiles, EXPERTS_PER_GROUP),
            in_specs=[
                pl.BlockSpec(memory_space=pl.ANY),
                pl.BlockSpec((1, D_MODEL, D_EXPERT), w_ix),
                pl.BlockSpec((1, D_MODEL, D_EXPERT), w_ix),
                pl.BlockSpec((1, D_EXPERT, D_MODEL), w_ix),
            ],
            out_specs=pl.BlockSpec((MOE_TILE, D_MODEL), lambda i, j, tg, src: (i, 0)),
            scratch_shapes=[
                pltpu.VMEM((2, MOE_TILE, ROW_WIDTH), F32),
                pltpu.SemaphoreType.DMA((2,)),
                pltpu.VMEM((MOE_TILE, D_MODEL), BF16),
                pltpu.VMEM((MOE_TILE, D_MODEL), F32),
            ],
        ),
        out_shape=jax.ShapeDtypeStruct((n_tiles * MOE_TILE, D_MODEL), F32),
        compiler_params=_cparams("arbitrary", "arbitrary"),
        name="moe_grouped_prompt",
    )(tile_group, src_of_sorted, rows, wg, wu, wd)


def _moe_finish_kernel(pos_ref, x1_ref, gf_ref, ys_hbm, y_ref, buf, sem, *, n_tiles):
    i = pl.program_id(0)
    slot = i % 2
    tm = x1_ref.shape[0]

    @pl.when(i == 0)
    def _():
        _gather_rows(pos_ref, 0, ys_hbm, buf.at[0], sem.at[0], tm)

    @pl.when(i + 1 < n_tiles)
    def _():
        _gather_rows(pos_ref, (i + 1) * tm, ys_hbm, buf.at[1 - slot], sem.at[1 - slot], tm)

    _wait_rows(ys_hbm, buf.at[slot], sem.at[slot], tm)
    x2 = x1_ref[...] + buf[slot]
    y_ref[...] = x2 * lax.rsqrt(jnp.mean(x2 * x2, axis=-1, keepdims=True) + RMS_EPS) * gf_ref[...]


def _moe_finish(pos_of_token, x1, gf, ys, tm):
    t = x1.shape[0]
    n_tiles = t // tm
    kern = functools.partial(_moe_finish_kernel, n_tiles=n_tiles)
    return pl.pallas_call(
        kern,
        grid_spec=pltpu.PrefetchScalarGridSpec(
            num_scalar_prefetch=1,
            grid=(n_tiles,),
            in_specs=[
                pl.BlockSpec((tm, D_MODEL), lambda i, pos: (i, 0)),
                pl.BlockSpec((1, D_MODEL), lambda i, pos: (0, 0)),
                pl.BlockSpec(memory_space=pl.ANY),
            ],
            out_specs=pl.BlockSpec((tm, D_MODEL), lambda i, pos: (i, 0)),
            scratch_shapes=[pltpu.VMEM((2, tm, D_MODEL), F32), pltpu.SemaphoreType.DMA((2,))],
        ),
        out_shape=jax.ShapeDtypeStruct((t, D_MODEL), F32),
        compiler_params=_cparams("arbitrary"),
        name="moe_finish_prompt",
    )(pos_of_token, x1, gf, ys)


def _sort_plan(group_id, n_tiles):
    t = group_id.shape[0]
    onehot = (group_id[:, None] == jnp.arange(N_EXPERT_GROUPS, dtype=jnp.int32)[None, :]).astype(jnp.int32)
    counts = jnp.sum(onehot, axis=0)
    rank = jnp.sum((jnp.cumsum(onehot, axis=0) - onehot) * onehot, axis=1)
    padded = (counts + MOE_TILE - 1) // MOE_TILE * MOE_TILE
    group_end = jnp.cumsum(padded)
    group_start = group_end - padded
    pos = group_start[group_id] + rank
    src = jnp.zeros((n_tiles * MOE_TILE,), jnp.int32).at[pos].set(jnp.arange(t, dtype=jnp.int32))
    tile_start = jnp.arange(n_tiles, dtype=jnp.int32) * MOE_TILE
    tile_group = jnp.sum((tile_start[:, None] >= group_end[None, :]).astype(jnp.int32), axis=1)
    return pos.astype(jnp.int32), src, tile_group.astype(jnp.int32)


def _post_mixer_grouped(x2d, attn, conv, w_out_b, g2, w_rt, b_rt, wg_b, wu_b, wd_b, gf):
    t = x2d.shape[0]
    n_tiles = t // MOE_TILE + N_EXPERT_GROUPS
    x1, rows = _mix_route(x2d, attn, conv, w_out_b, g2, w_rt, b_rt, TOKEN_TILE)
    group_id = rows[:, D_MODEL + GROUP_ID_COL].astype(jnp.int32)
    pos, src, tile_group = _sort_plan(group_id, n_tiles)
    ys = _moe_grouped(tile_group, src, rows, wg_b, wu_b, wd_b, n_tiles)
    return _moe_finish(pos, x1, gf, ys, TOKEN_TILE)


def _kmean_select_kernel(pt_ref, q_ref, ck_hbm, sel_ref, buf, sem, sums_sc, *, n_pages, n_seq):
    b = pl.program_id(0)
    n_blocks = n_pages // PAGES_PER_BLOCK
    page_flat = ATTN_WIDTH

    def page_copy(seq, p, slot):
        start = pl.multiple_of(pt_ref[seq * n_pages + p] * page_flat, page_flat)
        return pltpu.make_async_copy(ck_hbm.at[pl.ds(start, page_flat), :], buf.at[slot], sem.at[slot])

    @pl.when(b == 0)
    def _():
        for s in range(KMEAN_RING):
            page_copy(0, s, s).start()

    sums_sc[...] = jnp.zeros_like(sums_sc)
    lane = lax.broadcasted_iota(jnp.int32, (1, LANES), 1)
    sub = SUBLANES
    groups = HEAD_DIM // sub

    def block_body(n, carry):
        part = jnp.zeros((N_HEADS, sub, LANES), F32)
        for pg in range(PAGES_PER_BLOCK):
            p = n * PAGES_PER_BLOCK + pg
            slot = p % KMEAN_RING
            page_copy(b, p, slot).wait()
            prod = buf[slot] * q_ref[0]
            part = part + jnp.sum(prod.reshape(N_HEADS, groups, sub, LANES), axis=1)
            ahead = p + KMEAN_RING

            @pl.when(ahead < n_pages)
            def _():
                page_copy(b, ahead, slot).start()

            @pl.when((ahead >= n_pages) & (b + 1 < n_seq))
            def _():
                page_copy(b + 1, ahead - n_pages, slot).start()
        block_sum = jnp.sum(part.reshape(N_HEADS * sub, LANES), axis=1, keepdims=True)
        sums_sc[...] = jnp.where(lane == n, block_sum, sums_sc[...])
        return carry

    lax.fori_loop(0, n_blocks, block_body, 0)

    fold = (lax.broadcasted_iota(jnp.int32, (N_HEADS, N_HEADS * sub), 1) // sub
            == lax.broadcasted_iota(jnp.int32, (N_HEADS, N_HEADS * sub), 0)).astype(F32)
    gate = jnp.dot(fold, sums_sc[...], precision=HIGHEST, preferred_element_type=F32) * (1.0 / MOBA_BLOCK)
    blk = lax.broadcasted_iota(jnp.int32, gate.shape, 1)
    gate = jnp.where(blk < n_blocks, gate, NEG_INF)
    rank = jnp.zeros(gate.shape, jnp.int32)
    for s in range(1, LANES):
        other = pltpu.roll(gate, s, axis=1)
        other_blk = jnp.where(blk >= s, blk - s, blk - s + LANES)
        ahead = (other > gate) | ((other == gate) & (other_blk < blk))
        rank = rank + ahead.astype(jnp.int32)
    out = jnp.zeros(gate.shape, jnp.int32)
    for r in range(MOBA_TOPK):
        picked = jnp.sum(jnp.where(rank == r, blk, 0).astype(F32), axis=1, keepdims=True)
        out = jnp.where(blk == r, picked.astype(jnp.int32), out)
    sel_ref[0] = out


def _kmean_select(page_table_flat, q_lanes, cache_kt, n_seq, n_pages):
    kern = functools.partial(_kmean_select_kernel, n_pages=n_pages, n_seq=n_seq)
    return pl.pallas_call(
        kern,
        grid_spec=pltpu.PrefetchScalarGridSpec(
            num_scalar_prefetch=1,
            grid=(n_seq,),
            in_specs=[
                pl.BlockSpec((1, ATTN_WIDTH, LANES), lambda b, pt: (b, 0, 0)),
                pl.BlockSpec(memory_space=pl.ANY),
            ],
            out_specs=pl.BlockSpec((1, N_HEADS, LANES), lambda b, pt: (b, 0, 0)),
            scratch_shapes=[
                pltpu.VMEM((KMEAN_RING, ATTN_WIDTH, LANES), F32),
                pltpu.SemaphoreType.DMA((KMEAN_RING,)),
                pltpu.VMEM((N_HEADS * SUBLANES, LANES), F32),
            ],
        ),
        out_shape=jax.ShapeDtypeStruct((n_seq, N_HEADS, LANES), jnp.int32),
        compiler_params=_cparams("arbitrary"),
        name="kmean_select_sample",
    )(page_table_flat, q_lanes, cache_kt)


def _moba_sample_kernel(pt_ref, sel_ref, q_ref, kn_ref, vn_ref, ck_hbm, cv_hbm, o_ref,
                        kbuf, vbuf, ksem, vsem, *, n_pages, n_seq):
    b = pl.program_id(0)
    half = b % 2

    def seq_copies(seq, dst_half):
        copies = []
        for h in range(N_HEADS):
            for r in range(MOBA_TOPK):
                blk = sel_ref[(seq * N_HEADS + h) * MOBA_TOPK + r]
                for pg in range(PAGES_PER_BLOCK):
                    page = pt_ref[seq * n_pages + blk * PAGES_PER_BLOCK + pg]
                    src = pl.ds(pl.multiple_of(page * ATTN_WIDTH + h * HEAD_DIM, HEAD_DIM), HEAD_DIM)
                    dst = pl.ds((r * PAGES_PER_BLOCK + pg) * LANES, LANES)
                    copies.append(pltpu.make_async_copy(ck_hbm.at[src, :], kbuf.at[dst_half, h, :, dst],
                                                        ksem.at[dst_half]))
                    copies.append(pltpu.make_async_copy(cv_hbm.at[src, :], vbuf.at[dst_half, h, :, dst],
                                                        vsem.at[dst_half]))
        return copies

    @pl.when(b == 0)
    def _():
        for cp in seq_copies(0, 0):
            cp.start()

    @pl.when(b + 1 < n_seq)
    def _():
        for cp in seq_copies(b + 1, 1 - half):
            cp.start()

    for cp in seq_copies(b, half):
        cp.wait()

    q8 = q_ref[0]
    kn = kn_ref[0]
    vn = vn_ref[0]
    nt = (((1,), (1,)), ((), ()))
    l_past = jnp.concatenate(
        [jnp.dot(q8, kbuf[half, h], precision=HIGHEST, preferred_element_type=F32)[h:h + 1, :]
         for h in range(N_HEADS)], axis=0) * SM_SCALE
    l_self = jnp.sum(q8 * kn, axis=1, keepdims=True) * SM_SCALE
    m = jnp.maximum(jnp.max(l_past, axis=1, keepdims=True), l_self)
    p = jnp.exp(l_past - m)
    p_self = jnp.exp(l_self - m)
    den = jnp.sum(p, axis=1, keepdims=True) + p_self
    pv = jnp.concatenate(
        [lax.dot_general(p, vbuf[half, h], nt, precision=HIGHEST, preferred_element_type=F32)[h:h + 1, :]
         for h in range(N_HEADS)], axis=0)
    o_ref[0] = (p_self * vn + pv) / den


def _moba_sample(page_table_flat, sel_flat, q3, k3, v3, cache_kt, cache_vt, n_seq, n_pages):
    kern = functools.partial(_moba_sample_kernel, n_pages=n_pages, n_seq=n_seq)
    n_keys = MOBA_TOPK * MOBA_BLOCK
    tok = pl.BlockSpec((1, N_HEADS, HEAD_DIM), lambda b, pt, sel: (b, 0, 0))
    return pl.pallas_call(
        kern,
        grid_spec=pltpu.PrefetchScalarGridSpec(
            num_scalar_prefetch=2,
            grid=(n_seq,),
            in_specs=[tok, tok, tok, pl.BlockSpec(memory_space=pl.ANY), pl.BlockSpec(memory_space=pl.ANY)],
            out_specs=tok,
            scratch_shapes=[
                pltpu.VMEM((2, N_HEADS, HEAD_DIM, n_keys), F32),
                pltpu.VMEM((2, N_HEADS, HEAD_DIM, n_keys), F32),
                pltpu.SemaphoreType.DMA((2,)),
                pltpu.SemaphoreType.DMA((2,)),
            ],
        ),
        out_shape=jax.ShapeDtypeStruct((n_seq, N_HEADS, HEAD_DIM), F32),
        compiler_params=_cparams("arbitrary"),
        name="moba_sample",
    )(page_table_flat, sel_flat, q3, k3, v3, cache_kt, cache_vt)


def _token_tile(t, want):
    return want if t % want == 0 else t


def kernel(x_prompt, x_sample, cache_k, cache_v, state_conv, page_table, norm1_g, w_in, b_in, conv_w, conv_b,
           conv_ln_g, conv_ln_b, w_out, norm2_g, w_group, b_group, w_router, b_router, w_gate, w_up, w_down,
           norm_f_g):
    bsz, slen, _ = x_prompt.shape
    dbsz, dlen, _ = x_sample.shape
    depth = w_in.shape[0]
    assert depth == 1 and dlen == 1, "one layer and one new sample token per sequence"
    n_pages, page_rows = page_table.shape[1], cache_k.shape[2]
    assert MOBA_BLOCK == PAGES_PER_BLOCK * page_rows and slen % (MOBA_BLOCK * PAST_SPAN) == 0
    assert page_rows == LANES and MOBA_TOPK <= n_pages // PAGES_PER_BLOCK <= LANES and n_pages % KMEAN_RING == 0
    past_len = n_pages * page_rows
    l = 0

    w_in_b = w_in[l].astype(BF16)
    w_out_b = w_out[l].astype(BF16)
    wg_b, wu_b, wd_b = w_gate[l].astype(BF16), w_up[l].astype(BF16), w_down[l].astype(BF16)
    pad_cols = ROUTER_COLS - N_EXPERTS - N_EXPERT_GROUPS
    w_rt = jnp.pad(jnp.concatenate([w_router[l], w_group[l]], axis=1), ((0, 0), (0, pad_cols)))
    b_rt = jnp.pad(jnp.concatenate([b_router[l], b_group[l]])[None, :], ((0, 0), (0, pad_cols)))
    g1, g2, gf = norm1_g[l][None, :], norm2_g[l][None, :], norm_f_g[None, :]
    b_in2, cb = b_in[l][None, :], conv_b[l][None, :]
    lg, lb = conv_ln_g[l][None, :], conv_ln_b[l][None, :]

    t_p = bsz * slen
    xp = x_prompt.reshape(t_p, D_MODEL)
    tm_in = _token_tile(slen, 512)
    q_p, kt_p, vt_p, u_p, kb_p, vtb_p, km_p = _in_projection(xp, g1, w_in_b, b_in2, tm_in, True, 0, 1, slen)
    km_p = km_p.reshape(bsz, slen // MOBA_BLOCK, ATTN_WIDTH)
    attn_p = _moba_prompt(q_p, kb_p, vtb_p, km_p, bsz, slen)
    conv_p = _conv_prompt(u_p, conv_w[l], cb, lg, lb, bsz, slen, _token_tile(slen, 256))
    y_p = _post_mixer_grouped(xp, attn_p, conv_p, w_out_b, g2, w_rt, b_rt, wg_b, wu_b, wd_b, gf)

    xs = x_sample.reshape(dbsz, D_MODEL)
    q_s, k_s, v_s, u_s = _in_projection(xs, g1, w_in_b, b_in2, dbsz, False, past_len, 0, dbsz)
    heads = lambda t: t.reshape(dbsz, N_HEADS, HEAD_DIM)
    ck2 = cache_k[l].transpose(0, 2, 3, 1).reshape(-1, page_rows)
    cv2 = cache_v[l].transpose(0, 2, 3, 1).reshape(-1, page_rows)
    pt_flat = page_table.reshape(-1)
    q_lanes = jnp.broadcast_to(q_s[:, :, None], (dbsz, ATTN_WIDTH, LANES))
    sel = _kmean_select(pt_flat, q_lanes, ck2, dbsz, n_pages)
    sel_flat = sel[:, :, :MOBA_TOPK].reshape(-1)
    attn_s = _moba_sample(pt_flat, sel_flat, heads(q_s), heads(k_s), heads(v_s), ck2, cv2, dbsz, n_pages)
    conv_s = _conv_sample(state_conv[l], u_s, conv_w[l], cb, lg, lb)
    y_s = _post_mixer(xs, attn_s.reshape(dbsz, ATTN_WIDTH).astype(BF16), conv_s, w_out_b, g2, w_rt, b_rt,
                      wg_b, wu_b, wd_b, gf, dbsz, "post_mixer_sample")

    hist = CONV_LEN - 1
    kv_p = lambda t: t.reshape(bsz, N_HEADS, HEAD_DIM, slen).transpose(0, 3, 1, 2)[None]
    kv_s = lambda t: t.reshape(1, dbsz, 1, N_HEADS, HEAD_DIM)
    new_conv_p = u_p.reshape(bsz, slen, CONV_CH)[:, slen - hist:][None]
    new_conv_s = jnp.concatenate([state_conv[l][:, 1:], u_s[:, None, :]], axis=1)[None]
    return (y_p.reshape(bsz, slen, D_MODEL), y_s.reshape(dbsz, 1, D_MODEL), kv_p(kt_p), kv_p(vt_p), new_conv_p,
            kv_s(k_s), kv_s(v_s), new_conv_s)
```

```python
import functools
import math

import jax
import jax.numpy as jnp
import numpy as np
from jax import lax
from jax.experimental import pallas as pl
from jax.experimental.pallas import tpu as pltpu

F32 = jnp.float32
BF16 = jnp.bfloat16
HIGHEST = lax.Precision.HIGHEST

D_MODEL = 1024
ATTN_WIDTH = 512
CONV_CH = 512
HEAD_DIM = 64
N_HEADS = 8
IN_WIDTH = 3 * ATTN_WIDTH + 2 * CONV_CH
CONV_LEN = 31
MOBA_BLOCK = 256
MOBA_TOPK = 3
ROPE_THETA = 10000.0
N_EXPERT_GROUPS = 4
EXPERTS_PER_GROUP = 4
N_EXPERTS = 16
D_EXPERT = 512
RMS_EPS = 1e-6
LN_EPS = 1e-5
NEG_INF = -1e30
SM_SCALE = HEAD_DIM ** -0.5
_ROPE_LOG_STEP = -math.log(ROPE_THETA) / (HEAD_DIM // 2)
ROPE_LOG_STEP_HI = float(np.float32(_ROPE_LOG_STEP))
ROPE_LOG_STEP_LO = _ROPE_LOG_STEP - ROPE_LOG_STEP_HI

LANES = 128
SUBLANES = 8
HEADS_PER_LANE_TILE = LANES // HEAD_DIM
ROUTER_COLS = LANES
GROUP_COL0 = N_EXPERTS
VMEM_LIMIT = 56 * 1024 * 1024
CONV_HALO = 32
CONV_CHUNK = 64
LOG2_E = math.log2(math.e)
PAST_SPAN = 2
ATTN_TILES = 2
KMEAN_RING = 16
PAGES_PER_BLOCK = 2
MOE_TILE = 512
TOKEN_TILE = 512
GATHER_UNROLL = 8
ROW_WIDTH = D_MODEL + ROUTER_COLS
GROUP_ID_COL = ROUTER_COLS - 1


def _cparams(*sem):
    return pltpu.CompilerParams(dimension_semantics=sem, vmem_limit_bytes=VMEM_LIMIT)


def _inproj_kernel(x_ref, g_ref, w_ref, b_ref, q_ref, k_ref, v_ref, u_ref, *prompt_refs,
                   pos0, pos_stride, n_pos_tiles):
    x = x_ref[...]
    tm = x.shape[0]
    h = x * lax.rsqrt(jnp.mean(x * x, axis=-1, keepdims=True) + RMS_EPS) * g_ref[...]
    z = jnp.dot(h.astype(BF16), w_ref[...], preferred_element_type=F32) + b_ref[...]
    a = ATTN_WIDTH
    reps = a // LANES
    half = HEAD_DIM // 2
    lane_t = lax.broadcasted_iota(jnp.int32, (1, LANES), 1)
    freq_ix = (lane_t % half).astype(F32)
    inv_freq = jnp.exp(freq_ix * ROPE_LOG_STEP_HI) * jnp.exp(freq_ix * ROPE_LOG_STEP_LO)
    tile_pos = pos0 + (pl.program_id(0) % n_pos_tiles) * (tm * pos_stride)
    pos = tile_pos + lax.broadcasted_iota(jnp.int32, (tm, 1), 0) * pos_stride
    ang = pos.astype(F32) * inv_freq
    sin_sign = jnp.where((lane_t % HEAD_DIM) < half, -1.0, 1.0)
    cos = jnp.concatenate([jnp.cos(ang)] * reps, axis=-1)
    sin = jnp.concatenate([jnp.sin(ang) * sin_sign] * reps, axis=-1)
    lane = lax.broadcasted_iota(jnp.int32, (1, a), 1)
    first_half = (lane % HEAD_DIM) < half

    def rope(t):
        from_below = pltpu.roll(t, HEAD_DIM // 2, axis=1)
        from_above = pltpu.roll(t, a - HEAD_DIM // 2, axis=1)
        return t * cos + jnp.where(first_half, from_above, from_below) * sin

    q = rope(z[:, 0:a])
    k = rope(z[:, a:2 * a])
    v = z[:, 2 * a:3 * a]
    u = z[:, 3 * a:3 * a + CONV_CH] * jax.nn.sigmoid(z[:, 3 * a + CONV_CH:])
    q_ref[...] = q
    u_ref[...] = u
    if prompt_refs:
        kb_ref, vtb_ref, km_ref = prompt_refs
        v_t = v.T
        k_ref[...] = k.T
        v_ref[...] = v_t
        kb_ref[...] = k.astype(BF16)
        vtb_ref[...] = v_t.astype(BF16)
        nblk = k.shape[0] // MOBA_BLOCK
        km_ref[0] = jnp.mean(k.reshape(nblk, MOBA_BLOCK, a), axis=1)
    else:
        k_ref[...] = k
        v_ref[...] = v


def _in_projection(x2d, norm_g, w_bf16, b_in, tm, prompt, pos0, pos_stride, seq_len):
    t = x2d.shape[0]
    n_t = t // tm
    kern = functools.partial(_inproj_kernel, pos0=pos0, pos_stride=pos_stride, n_pos_tiles=seq_len // tm)
    row = lambda i: (i, 0)
    fixed = lambda i: (0, 0)
    n_pos = seq_len // tm
    wide = jax.ShapeDtypeStruct((t, ATTN_WIDTH), F32)
    wide_spec = pl.BlockSpec((tm, ATTN_WIDTH), row)
    if prompt:
        nblk = tm // MOBA_BLOCK
        n_seq = t // seq_len
        feat_spec = pl.BlockSpec((ATTN_WIDTH, tm), lambda i: (i // n_pos, i % n_pos))
        feat_f32 = jax.ShapeDtypeStruct((n_seq * ATTN_WIDTH, seq_len), F32)
        feat_bf16 = jax.ShapeDtypeStruct((n_seq * ATTN_WIDTH, seq_len), BF16)
        out_shape = [wide, feat_f32, feat_f32, wide, jax.ShapeDtypeStruct((t, ATTN_WIDTH), BF16), feat_bf16,
                     jax.ShapeDtypeStruct((n_t, nblk, ATTN_WIDTH), F32)]
        out_specs = [wide_spec, feat_spec, feat_spec, wide_spec, wide_spec, feat_spec,
                     pl.BlockSpec((1, nblk, ATTN_WIDTH), lambda i: (i, 0, 0))]
    else:
        out_shape = [wide] * 4
        out_specs = [wide_spec] * 4
    return pl.pallas_call(
        kern,
        grid=(n_t,),
        in_specs=[
            pl.BlockSpec((tm, D_MODEL), row),
            pl.BlockSpec((1, D_MODEL), fixed),
            pl.BlockSpec((D_MODEL, IN_WIDTH), fixed),
            pl.BlockSpec((1, IN_WIDTH), fixed),
        ],
        out_specs=out_specs,
        out_shape=out_shape,
        compiler_params=_cparams("parallel"),
        name="in_projection_prompt" if prompt else "in_projection_sample",
    )(x2d, norm_g, w_bf16, b_in)


def _moba_prompt_kernel(q_ref, kb_ref, vt_ref, km_ref, o_ref, m_sc, l_sc, acc_sc, bias_sc, s_even, s_odd):
    qi = pl.program_id(2)
    blk_rows = MOBA_BLOCK
    span_rows = PAST_SPAN * blk_rows
    tiles = range(ATTN_TILES)
    cols = lambda t: slice(t * LANES, (t + 1) * LANES)
    nt = (((1,), (1,)), ((), ()))
    lane = lax.broadcasted_iota(jnp.int32, (1, LANES), 1)
    head0 = lane < HEAD_DIM
    n_blocks = km_ref.shape[1]

    qb = []
    for t in tiles:
        q = q_ref[:, cols(t)]
        q2 = jnp.concatenate([jnp.where(head0, q, 0.0), jnp.where(head0, 0.0, q)], axis=0)
        km = km_ref[0, :, cols(t)]
        km_hi, q_hi = km.astype(BF16), q2.astype(BF16)
        km_lo, q_lo = (km - km_hi.astype(F32)).astype(BF16), (q2 - q_hi.astype(F32)).astype(BF16)
        gate = (lax.dot_general(km_hi, q_hi, nt, preferred_element_type=F32)
                + lax.dot_general(km_lo, q_hi, nt, preferred_element_type=F32)
                + lax.dot_general(km_hi, q_lo, nt, preferred_element_type=F32))
        blk = lax.broadcasted_iota(jnp.int32, gate.shape, 0)
        fully_past = blk < qi
        gate = jnp.where(fully_past, gate, NEG_INF)
        rank = jnp.zeros(gate.shape, jnp.int32)
        for m in range(n_blocks):
            gm = gate[m:m + 1, :]
            ahead = (gm > gate) | ((gm == gate) & (m < blk))
            rank = rank + ahead.astype(jnp.int32)
        bias_sc[t] = jnp.where(fully_past & (rank < MOBA_TOPK), 0.0, NEG_INF)
        qb.append((q2 * (SM_SCALE * LOG2_E)).astype(BF16))

    def scores(t, j0):
        r0 = pl.multiple_of(j0 * blk_rows, blk_rows)
        s = lax.dot_general(kb_ref[pl.ds(r0, span_rows), cols(t)], qb[t], nt, preferred_element_type=F32)
        return jnp.concatenate([s[i * blk_rows:(i + 1) * blk_rows] + bias_sc[t, pl.ds(j0 + i, 1), :]
                                for i in range(PAST_SPAN)], axis=0)

    def fold(t, s, j0):
        r0 = pl.multiple_of(j0 * blk_rows, blk_rows)
        m_old = m_sc[t]
        m_new = jnp.maximum(m_old, jnp.max(s, axis=0, keepdims=True))
        alpha = jnp.exp2(m_old - m_new)
        p = jnp.exp2(s - m_new)
        l_sc[t] = alpha * l_sc[t] + jnp.sum(p, axis=0, keepdims=True)
        acc_sc[t] = alpha * acc_sc[t] + jnp.dot(vt_ref[cols(t), pl.ds(r0, span_rows)], p.astype(BF16),
                                                preferred_element_type=F32)
        m_sc[t] = m_new

    n_spans = (qi + PAST_SPAN - 1) // PAST_SPAN
    last_span = jnp.maximum(n_spans - 1, 0)
    span0 = lambda i: jnp.minimum(i, last_span) * PAST_SPAN
    for t in tiles:
        s_even[t] = scores(t, 0)

    own0 = pl.multiple_of(qi * blk_rows, blk_rows)
    for t in tiles:
        s = lax.dot_general(kb_ref[pl.ds(own0, blk_rows), cols(t)], qb[t], nt, preferred_element_type=F32)
        k_ix = lax.broadcasted_iota(jnp.int32, s.shape, 0)
        q_ix = lax.broadcasted_iota(jnp.int32, s.shape, 1) % blk_rows
        s = jnp.where(k_ix <= q_ix, s, NEG_INF)
        m0 = jnp.max(s, axis=0, keepdims=True)
        p = jnp.exp2(s - m0)
        m_sc[t] = m0
        l_sc[t] = jnp.sum(p, axis=0, keepdims=True)
        acc_sc[t] = jnp.dot(vt_ref[cols(t), pl.ds(own0, blk_rows)], p.astype(BF16), preferred_element_type=F32)

    def pair_body(u, carry):
        i = 2 * u
        for t in tiles:
            s_odd[t] = scores(t, span0(i + 1))
        for t in tiles:
            fold(t, s_even[t], i * PAST_SPAN)
        for t in tiles:
            s_even[t] = scores(t, span0(i + 2))
        for t in tiles:
            fold(t, s_odd[t], (i + 1) * PAST_SPAN)
        return carry

    lax.fori_loop(0, n_spans // 2, pair_body, 0)

    @pl.when(n_spans % 2 == 1)
    def _():
        for t in tiles:
            fold(t, s_even[t], (n_spans - 1) * PAST_SPAN)

    for t in tiles:
        out = acc_sc[t] / l_sc[t]
        out_t = jnp.concatenate([out[:HEAD_DIM, :blk_rows], out[HEAD_DIM:, blk_rows:]], axis=0)
        o_ref[:, cols(t)] = out_t.T.astype(o_ref.dtype)


def _moba_prompt(q, kb, vtb, kmean, bsz, slen):
    n_q = slen // MOBA_BLOCK
    width = ATTN_TILES * LANES
    n_w = ATTN_WIDTH // width
    cols2 = HEADS_PER_LANE_TILE * MOBA_BLOCK
    return pl.pallas_call(
        _moba_prompt_kernel,
        grid=(bsz, n_w, n_q),
        in_specs=[
            pl.BlockSpec((MOBA_BLOCK, width), lambda b, w, qi: (b * n_q + qi, w)),
            pl.BlockSpec((slen, width), lambda b, w, qi: (b, w)),
            pl.BlockSpec((width, slen), lambda b, w, qi: (b * n_w + w, 0)),
            pl.BlockSpec((1, n_q, width), lambda b, w, qi: (b, 0, w)),
        ],
        out_specs=pl.BlockSpec((MOBA_BLOCK, width), lambda b, w, qi: (b * n_q + qi, w)),
        out_shape=jax.ShapeDtypeStruct((bsz * slen, ATTN_WIDTH), BF16),
        scratch_shapes=[
            pltpu.VMEM((ATTN_TILES, 1, cols2), F32),
            pltpu.VMEM((ATTN_TILES, 1, cols2), F32),
            pltpu.VMEM((ATTN_TILES, LANES, cols2), F32),
            pltpu.VMEM((ATTN_TILES, n_q, cols2), F32),
            pltpu.VMEM((ATTN_TILES, PAST_SPAN * MOBA_BLOCK, cols2), F32),
            pltpu.VMEM((ATTN_TILES, PAST_SPAN * MOBA_BLOCK, cols2), F32),
        ],
        compiler_params=_cparams("parallel", "parallel", "arbitrary"),
        name="moba_prompt",
    )(q, kb, vtb, kmean)


def _ln_swish(c, g, b):
    mu = jnp.mean(c, axis=-1, keepdims=True)
    d = c - mu
    var = jnp.mean(d * d, axis=-1, keepdims=True)
    y = d * lax.rsqrt(var + LN_EPS) * g + b
    return y * jax.nn.sigmoid(y)


def _conv_prompt_kernel(prev_ref, cur_ref, w_ref, cb_ref, g_ref, b_ref, o_ref, ext_sc):
    i = pl.program_id(1)
    tc = cur_ref.shape[0]
    ext_sc[0:CONV_HALO, :] = jnp.where(i > 0, prev_ref[...], 0.0)
    ext_sc[CONV_HALO:CONV_HALO + tc, :] = cur_ref[...]
    ext_sc[CONV_HALO + tc:CONV_HALO + tc + SUBLANES, :] = jnp.zeros((SUBLANES, CONV_CH), F32)
    lead = CONV_HALO - (CONV_LEN - 1)
    for r0 in range(0, tc, CONV_CHUNK):
        acc = jnp.zeros((CONV_CHUNK, CONV_CH), F32)
        for res in range(SUBLANES):
            taps = [j for j in range(CONV_LEN) if (j + lead) % SUBLANES == res]
            part = None
            for j in taps:
                base = r0 + j + lead - res
                term = w_ref[j:j + 1, :] * ext_sc[base:base + CONV_CHUNK + SUBLANES, :]
                part = term if part is None else part + term
            if part is not None:
                acc = acc + part[res:res + CONV_CHUNK]
        y = _ln_swish(acc + cb_ref[...], g_ref[...], b_ref[...])
        o_ref[r0:r0 + CONV_CHUNK, :] = y.astype(o_ref.dtype)


def _conv_prompt(u2d, conv_w, conv_b, ln_g, ln_b, bsz, slen, tc):
    n_c = slen // tc
    halo_per_tile = tc // CONV_HALO
    fixed = lambda b, i: (0, 0)
    return pl.pallas_call(
        _conv_prompt_kernel,
        grid=(bsz, n_c),
        in_specs=[
            pl.BlockSpec((CONV_HALO, CONV_CH),
                         lambda b, i: (jnp.maximum((b * n_c + i) * halo_per_tile - 1, 0), 0)),
            pl.BlockSpec((tc, CONV_CH), lambda b, i: (b * n_c + i, 0)),
            pl.BlockSpec((CONV_LEN, CONV_CH), fixed),
            pl.BlockSpec((1, CONV_CH), fixed),
            pl.BlockSpec((1, CONV_CH), fixed),
            pl.BlockSpec((1, CONV_CH), fixed),
        ],
        out_specs=pl.BlockSpec((tc, CONV_CH), lambda b, i: (b * n_c + i, 0)),
        out_shape=jax.ShapeDtypeStruct((bsz * slen, CONV_CH), BF16),
        scratch_shapes=[pltpu.VMEM((CONV_HALO + tc + SUBLANES, CONV_CH), F32)],
        compiler_params=_cparams("parallel", "parallel"),
        name="conv_prompt",
    )(u2d, u2d, conv_w, conv_b, ln_g, ln_b)


def _conv_sample_kernel(st_ref, u_ref, w_ref, cb_ref, g_ref, b_ref, o_ref):
    hist = CONV_LEN - 1
    for b in range(st_ref.shape[0]):
        u_row = u_ref[b:b + 1, :]
        c = jnp.sum(st_ref[b] * w_ref[0:hist, :], axis=0, keepdims=True) + w_ref[hist:hist + 1, :] * u_row
        y = _ln_swish(c + cb_ref[...], g_ref[...], b_ref[...])
        o_ref[b:b + 1, :] = y.astype(o_ref.dtype)


def _conv_sample(state, u2d, conv_w, conv_b, ln_g, ln_b):
    n = u2d.shape[0]
    return pl.pallas_call(
        _conv_sample_kernel,
        out_shape=jax.ShapeDtypeStruct((n, CONV_CH), BF16),
        compiler_params=pltpu.CompilerParams(vmem_limit_bytes=VMEM_LIMIT),
        name="conv_sample",
    )(state, u2d, conv_w, conv_b, ln_g, ln_b)


def _route(logits, with_group=False):
    c = lax.broadcasted_iota(jnp.int32, logits.shape, 1)
    big = jnp.int32(ROUTER_COLS)
    is_group = (c >= GROUP_COL0) & (c < GROUP_COL0 + N_EXPERT_GROUPS)
    gl = jnp.where(is_group, logits, NEG_INF)
    gmax = jnp.max(gl, axis=1, keepdims=True)
    g_sel = jnp.min(jnp.where(gl == gmax, c - GROUP_COL0, big), axis=1, keepdims=True)
    g_w = 1.0 / jnp.sum(jnp.exp(gl - gmax), axis=1, keepdims=True)
    in_group = (c < N_EXPERTS) & ((c // EXPERTS_PER_GROUP) == g_sel)
    el = jnp.where(in_group, logits, NEG_INF)
    m1 = jnp.max(el, axis=1, keepdims=True)
    i1 = jnp.min(jnp.where(el == m1, c, big), axis=1, keepdims=True)
    el2 = jnp.where(c == i1, NEG_INF, el)
    m2 = jnp.max(el2, axis=1, keepdims=True)
    i2 = jnp.min(jnp.where(el2 == m2, c, big), axis=1, keepdims=True)
    e21 = jnp.exp(m2 - m1)
    inv = 1.0 / (1.0 + e21)
    comb = g_w * jnp.where(c == i1, inv, jnp.where(c == i2, e21 * inv, 0.0))
    return (comb, g_sel) if with_group else comb


def _post_kernel(x_ref, a_ref, c_ref, wo_ref, g2_ref, wr_ref, br_ref, wg_ref, wu_ref, wd_ref, gf_ref,
                 y_ref, x1_sc, h2_sc, comb_sc, acc_sc):
    e = pl.program_id(1)

    @pl.when(e == 0)
    def _():
        x1 = (x_ref[...]
              + jnp.dot(a_ref[...], wo_ref[0:ATTN_WIDTH, :], preferred_element_type=F32)
              + jnp.dot(c_ref[...], wo_ref[ATTN_WIDTH:D_MODEL, :], preferred_element_type=F32))
        x1_sc[...] = x1
        h2 = x1 * lax.rsqrt(jnp.mean(x1 * x1, axis=-1, keepdims=True) + RMS_EPS) * g2_ref[...]
        h2_sc[...] = h2.astype(BF16)
        logits = jnp.dot(h2, wr_ref[...], precision=HIGHEST, preferred_element_type=F32) + br_ref[...]
        comb_sc[...] = _route(logits)
        acc_sc[...] = jnp.zeros_like(acc_sc)

    h2b = h2_sc[...]
    col = lax.broadcasted_iota(jnp.int32, comb_sc.shape, 1)
    ce = jnp.sum(jnp.where(col == e, comb_sc[...], 0.0), axis=1, keepdims=True)
    gt = jnp.dot(h2b, wg_ref[0], preferred_element_type=F32)
    up = jnp.dot(h2b, wu_ref[0], preferred_element_type=F32)
    hid = gt * jax.nn.sigmoid(gt) * up * ce
    acc_sc[...] += jnp.dot(hid.astype(BF16), wd_ref[0], preferred_element_type=F32)

    @pl.when(e == N_EXPERTS - 1)
    def _():
        x2 = x1_sc[...] + acc_sc[...]
        y_ref[...] = x2 * lax.rsqrt(jnp.mean(x2 * x2, axis=-1, keepdims=True) + RMS_EPS) * gf_ref[...]


def _post_mixer(x2d, attn, conv, w_out, g2, w_rt, b_rt, wg, wu, wd, gf, tm, name):
    t = x2d.shape[0]
    row = lambda i, e: (i, 0)
    fixed = lambda i, e: (0, 0)
    return pl.pallas_call(
        _post_kernel,
        grid=(t // tm, N_EXPERTS),
        in_specs=[
            pl.BlockSpec((tm, D_MODEL), row),
            pl.BlockSpec((tm, ATTN_WIDTH), row),
            pl.BlockSpec((tm, CONV_CH), row),
            pl.BlockSpec((D_MODEL, D_MODEL), fixed),
            pl.BlockSpec((1, D_MODEL), fixed),
            pl.BlockSpec((D_MODEL, ROUTER_COLS), fixed),
            pl.BlockSpec((1, ROUTER_COLS), fixed),
            pl.BlockSpec((1, D_MODEL, D_EXPERT), lambda i, e: (e, 0, 0)),
            pl.BlockSpec((1, D_MODEL, D_EXPERT), lambda i, e: (e, 0, 0)),
            pl.BlockSpec((1, D_EXPERT, D_MODEL), lambda i, e: (e, 0, 0)),
            pl.BlockSpec((1, D_MODEL), fixed),
        ],
        out_specs=pl.BlockSpec((tm, D_MODEL), row),
        out_shape=jax.ShapeDtypeStruct((t, D_MODEL), F32),
        scratch_shapes=[
            pltpu.VMEM((tm, D_MODEL), F32),
            pltpu.VMEM((tm, D_MODEL), BF16),
            pltpu.VMEM((tm, ROUTER_COLS), F32),
            pltpu.VMEM((tm, D_MODEL), F32),
        ],
        compiler_params=_cparams("parallel", "arbitrary"),
        name=name,
    )(x2d, attn, conv, w_out, g2, w_rt, b_rt, wg, wu, wd, gf)


def _mix_route_kernel(x_ref, a_ref, c_ref, wo_ref, g2_ref, wr_ref, br_ref, x1_ref, row_ref):
    x1 = (x_ref[...]
          + jnp.dot(a_ref[...], wo_ref[0:ATTN_WIDTH, :], preferred_element_type=F32)
          + jnp.dot(c_ref[...], wo_ref[ATTN_WIDTH:D_MODEL, :], preferred_element_type=F32))
    x1_ref[...] = x1
    h2 = x1 * lax.rsqrt(jnp.mean(x1 * x1, axis=-1, keepdims=True) + RMS_EPS) * g2_ref[...]
    h_hi = h2.astype(BF16)
    h_lo = (h2 - h_hi.astype(F32)).astype(BF16)
    w = wr_ref[...]
    w_hi = w.astype(BF16)
    w_lo = (w - w_hi.astype(F32)).astype(BF16)
    logits = (jnp.dot(h_hi, w_hi, preferred_element_type=F32) + jnp.dot(h_lo, w_hi, preferred_element_type=F32)
              + jnp.dot(h_hi, w_lo, preferred_element_type=F32) + br_ref[...])
    comb, g_sel = _route(logits, with_group=True)
    col = lax.broadcasted_iota(jnp.int32, comb.shape, 1)
    row_ref[:, 0:D_MODEL] = h2
    row_ref[:, D_MODEL:ROW_WIDTH] = jnp.where(col == GROUP_ID_COL, g_sel.astype(F32), comb)


def _mix_route(x2d, attn, conv, w_out, g2, w_rt, b_rt, tm):
    t = x2d.shape[0]
    row = lambda i: (i, 0)
    fixed = lambda i: (0, 0)
    return pl.pallas_call(
        _mix_route_kernel,
        grid=(t // tm,),
        in_specs=[
            pl.BlockSpec((tm, D_MODEL), row),
            pl.BlockSpec((tm, ATTN_WIDTH), row),
            pl.BlockSpec((tm, CONV_CH), row),
            pl.BlockSpec((D_MODEL, D_MODEL), fixed),
            pl.BlockSpec((1, D_MODEL), fixed),
            pl.BlockSpec((D_MODEL, ROUTER_COLS), fixed),
            pl.BlockSpec((1, ROUTER_COLS), fixed),
        ],
        out_specs=[pl.BlockSpec((tm, D_MODEL), row), pl.BlockSpec((tm, ROW_WIDTH), row)],
        out_shape=[jax.ShapeDtypeStruct((t, D_MODEL), F32), jax.ShapeDtypeStruct((t, ROW_WIDTH), F32)],
        compiler_params=_cparams("parallel"),
        name="mix_route_prompt",
    )(x2d, attn, conv, w_out, g2, w_rt, b_rt)


def _gather_rows(idx_ref, base, src_hbm, dst, sem, n_rows):
    def body(c, carry):
        for k in range(GATHER_UNROLL):
            r = c * GATHER_UNROLL + k
            pltpu.make_async_copy(src_hbm.at[pl.ds(idx_ref[base + r], 1), :], dst.at[pl.ds(r, 1), :], sem).start()
        return carry
    lax.fori_loop(0, n_rows // GATHER_UNROLL, body, 0)


def _wait_rows(src_hbm, dst, sem, n_rows):
    def body(c, carry):
        for k in range(GATHER_UNROLL):
            r = c * GATHER_UNROLL + k
            pltpu.make_async_copy(src_hbm.at[pl.ds(0, 1), :], dst.at[pl.ds(r, 1), :], sem).wait()
        return carry
    lax.fori_loop(0, n_rows // GATHER_UNROLL, body, 0)


def _moe_grouped_kernel(tg_ref, src_ref, rows_hbm, wg_ref, wu_ref, wd_ref, y_ref, xbuf, sem, hb_sc, acc_sc,
                        *, n_tiles):
    i = pl.program_id(0)
    j = pl.program_id(1)
    slot = i % 2
    group = tg_ref[i]

    @pl.when(j == 0)
    def _():
        @pl.when(i == 0)
        def _():
            _gather_rows(src_ref, 0, rows_hbm, xbuf.at[0], sem.at[0], MOE_TILE)

        @pl.when(i + 1 < n_tiles)
        def _():
            _gather_rows(src_ref, (i + 1) * MOE_TILE, rows_hbm, xbuf.at[1 - slot], sem.at[1 - slot], MOE_TILE)

        _wait_rows(rows_hbm, xbuf.at[slot], sem.at[slot], MOE_TILE)
        hb_sc[...] = xbuf[slot, :, 0:D_MODEL].astype(BF16)
        acc_sc[...] = jnp.zeros_like(acc_sc)

    @pl.when(group < N_EXPERT_GROUPS)
    def _():
        e = group * EXPERTS_PER_GROUP + j
        comb = xbuf[slot, :, D_MODEL:ROW_WIDTH]
        col = lax.broadcasted_iota(jnp.int32, comb.shape, 1)
        ce = jnp.sum(jnp.where(col == e, comb, 0.0), axis=1, keepdims=True)
        hb = hb_sc[...]
        gt = jnp.dot(hb, wg_ref[0], preferred_element_type=F32)
        up = jnp.dot(hb, wu_ref[0], preferred_element_type=F32)
        hid = gt * jax.nn.sigmoid(gt) * up * ce
        acc_sc[...] += jnp.dot(hid.astype(BF16), wd_ref[0], preferred_element_type=F32)

    @pl.when(j == EXPERTS_PER_GROUP - 1)
    def _():
        y_ref[...] = acc_sc[...]


def _moe_grouped(tile_group, src_of_sorted, rows, wg, wu, wd, n_tiles):
    kern = functools.partial(_moe_grouped_kernel, n_tiles=n_tiles)
    last = N_EXPERT_GROUPS - 1
    w_ix = lambda i, j, tg, src: (jnp.minimum(tg[i], last) * EXPERTS_PER_GROUP + j, 0, 0)
    return pl.pallas_call(
        kern,
        grid_spec=pltpu.PrefetchScalarGridSpec(
            num_scalar_prefetch=2,
            grid=(n_tiles, EXPERTS_PER_GROUP),
            in_specs=[
                pl.BlockSpec(memory_space=pl.ANY),
                pl.BlockSpec((1, D_MODEL, D_EXPERT), w_ix),
                pl.BlockSpec((1, D_MODEL, D_EXPERT), w_ix),
                pl.BlockSpec((1, D_EXPERT, D_MODEL), w_ix),
            ],
            out_specs=pl.BlockSpec((MOE_TILE, D_MODEL), lambda i, j, tg, src: (i, 0)),
            scratch_shapes=[
                pltpu.VMEM((2, MOE_TILE, ROW_WIDTH), F32),
                pltpu.SemaphoreType.DMA((2,)),
                pltpu.VMEM((MOE_TILE, D_MODEL), BF16),
                pltpu.VMEM((MOE_TILE, D_MODEL), F32),
            ],
        ),
        out_shape=jax.ShapeDtypeStruct((n_tiles * MOE_TILE, D_MODEL), F32),
        compiler_params=_cparams("arbitrary", "arbitrary"),
        name="moe_grouped_prompt",
    )(tile_group, src_of_sorted, rows, wg, wu, wd)


def _moe_finish_kernel(pos_ref, x1_ref, gf_ref, ys_hbm, y_ref, buf, sem, *, n_tiles):
    i = pl.program_id(0)
    slot = i % 2
    tm = x1_ref.shape[0]

    @pl.when(i == 0)
    def _():
        _gather_rows(pos_ref, 0, ys_hbm, buf.at[0], sem.at[0], tm)

    @pl.when(i + 1 < n_tiles)
    def _():
        _gather_rows(pos_ref, (i + 1) * tm, ys_hbm, buf.at[1 - slot], sem.at[1 - slot], tm)

    _wait_rows(ys_hbm, buf.at[slot], sem.at[slot], tm)
    x2 = x1_ref[...] + buf[slot]
    y_ref[...] = x2 * lax.rsqrt(jnp.mean(x2 * x2, axis=-1, keepdims=True) + RMS_EPS) * gf_ref[...]


def _moe_finish(pos_of_token, x1, gf, ys, tm):
    t = x1.shape[0]
    n_tiles = t // tm
    kern = functools.partial(_moe_finish_kernel, n_tiles=n_tiles)
    return pl.pallas_call(
        kern,
        grid_spec=pltpu.PrefetchScalarGridSpec(
            num_scalar_prefetch=1,
            grid=(n_tiles,),
            in_specs=[
                pl.BlockSpec((tm, D_MODEL), lambda i, pos: (i, 0)),
                pl.BlockSpec((1, D_MODEL), lambda i, pos: (0, 0)),
                pl.BlockSpec(memory_space=pl.ANY),
            ],
            out_specs=pl.BlockSpec((tm, D_MODEL), lambda i, pos: (i, 0)),
            scratch_shapes=[pltpu.VMEM((2, tm, D_MODEL), F32), pltpu.SemaphoreType.DMA((2,))],
        ),
        out_shape=jax.ShapeDtypeStruct((t, D_MODEL), F32),
        compiler_params=_cparams("arbitrary"),
        name="moe_finish_prompt",
    )(pos_of_token, x1, gf, ys)


def _sort_plan(group_id, n_tiles):
    t = group_id.shape[0]
    onehot = (group_id[:, None] == jnp.arange(N_EXPERT_GROUPS, dtype=jnp.int32)[None, :]).astype(jnp.int32)
    counts = jnp.sum(onehot, axis=0)
    rank = jnp.sum((jnp.cumsum(onehot, axis=0) - onehot) * onehot, axis=1)
    padded = (counts + MOE_TILE - 1) // MOE_TILE * MOE_TILE
    group_end = jnp.cumsum(padded)
    group_start = group_end - padded
    pos = group_start[group_id] + rank
    src = jnp.zeros((n_tiles * MOE_TILE,), jnp.int32).at[pos].set(jnp.arange(t, dtype=jnp.int32))
    tile_start = jnp.arange(n_tiles, dtype=jnp.int32) * MOE_TILE
    tile_group = jnp.sum((tile_start[:, None] >= group_end[None, :]).astype(jnp.int32), axis=1)
    return pos.astype(jnp.int32), src, tile_group.astype(jnp.int32)


def _post_mixer_grouped(x2d, attn, conv, w_out_b, g2, w_rt, b_rt, wg_b, wu_b, wd_b, gf):
    t = x2d.shape[0]
    n_tiles = t // MOE_TILE + N_EXPERT_GROUPS
    x1, rows = _mix_route(x2d, attn, conv, w_out_b, g2, w_rt, b_rt, TOKEN_TILE)
    group_id = rows[:, D_MODEL + GROUP_ID_COL].astype(jnp.int32)
    pos, src, tile_group = _sort_plan(group_id, n_tiles)
    ys = _moe_grouped(tile_group, src, rows, wg_b, wu_b, wd_b, n_tiles)
    return _moe_finish(pos, x1, gf, ys, TOKEN_TILE)


def _kmean_select_kernel(pt_ref, q_ref, ck_hbm, sel_ref, buf, sem, sums_sc, *, n_pages, n_seq):
    b = pl.program_id(0)
    n_blocks = n_pages // PAGES_PER_BLOCK
    page_flat = ATTN_WIDTH

    def page_copy(seq, p, slot):
        start = pl.multiple_of(pt_ref[seq * n_pages + p] * page_flat, page_flat)
        return pltpu.make_async_copy(ck_hbm.at[pl.ds(start, page_flat), :], buf.at[slot], sem.at[slot])

    @pl.when(b == 0)
    def _():
        for s in range(KMEAN_RING):
            page_copy(0, s, s).start()

    sums_sc[...] = jnp.zeros_like(sums_sc)
    lane = lax.broadcasted_iota(jnp.int32, (1, LANES), 1)
    sub = SUBLANES
    groups = HEAD_DIM // sub

    def block_body(n, carry):
        part = jnp.zeros((N_HEADS, sub, LANES), F32)
        for pg in range(PAGES_PER_BLOCK):
            p = n * PAGES_PER_BLOCK + pg
            slot = p % KMEAN_RING
            page_copy(b, p, slot).wait()
            prod = buf[slot] * q_ref[0]
            part = part + jnp.sum(prod.reshape(N_HEADS, groups, sub, LANES), axis=1)
            ahead = p + KMEAN_RING

            @pl.when(ahead < n_pages)
            def _():
                page_copy(b, ahead, slot).start()

            @pl.when((ahead >= n_pages) & (b + 1 < n_seq))
            def _():
                page_copy(b + 1, ahead - n_pages, slot).start()
        block_sum = jnp.sum(part.reshape(N_HEADS * sub, LANES), axis=1, keepdims=True)
        sums_sc[...] = jnp.where(lane == n, block_sum, sums_sc[...])
        return carry

    lax.fori_loop(0, n_blocks, block_body, 0)

    fold = (lax.broadcasted_iota(jnp.int32, (N_HEADS, N_HEADS * sub), 1) // sub
            == lax.broadcasted_iota(jnp.int32, (N_HEADS, N_HEADS * sub), 0)).astype(F32)
    gate = jnp.dot(fold, sums_sc[...], precision=HIGHEST, preferred_element_type=F32) * (1.0 / MOBA_BLOCK)
    blk = lax.broadcasted_iota(jnp.int32, gate.shape, 1)
    gate = jnp.where(blk < n_blocks, gate, NEG_INF)
    rank = jnp.zeros(gate.shape, jnp.int32)
    for s in range(1, LANES):
        other = pltpu.roll(gate, s, axis=1)
        other_blk = jnp.where(blk >= s, blk - s, blk - s + LANES)
        ahead = (other > gate) | ((other == gate) & (other_blk < blk))
        rank = rank + ahead.astype(jnp.int32)
    out = jnp.zeros(gate.shape, jnp.int32)
    for r in range(MOBA_TOPK):
        picked = jnp.sum(jnp.where(rank == r, blk, 0).astype(F32), axis=1, keepdims=True)
        out = jnp.where(blk == r, picked.astype(jnp.int32), out)
    sel_ref[0] = out


def _kmean_select(page_table_flat, q_lanes, cache_kt, n_seq, n_pages):
    kern = functools.partial(_kmean_select_kernel, n_pages=n_pages, n_seq=n_seq)
    return pl.pallas_call(
        kern,
        grid_spec=pltpu.PrefetchScalarGridSpec(
            num_scalar_prefetch=1,
            grid=(n_seq,),
            in_specs=[
                pl.BlockSpec((1, ATTN_WIDTH, LANES), lambda b, pt: (b, 0, 0)),
                pl.BlockSpec(memory_space=pl.ANY),
            ],
            out_specs=pl.BlockSpec((1, N_HEADS, LANES), lambda b, pt: (b, 0, 0)),
            scratch_shapes=[
                pltpu.VMEM((KMEAN_RING, ATTN_WIDTH, LANES), F32),
                pltpu.SemaphoreType.DMA((KMEAN_RING,)),
                pltpu.VMEM((N_HEADS * SUBLANES, LANES), F32),
            ],
        ),
        out_shape=jax.ShapeDtypeStruct((n_seq, N_HEADS, LANES), jnp.int32),
        compiler_params=_cparams("arbitrary"),
        name="kmean_select_sample",
    )(page_table_flat, q_lanes, cache_kt)


def _moba_sample_kernel(pt_ref, sel_ref, q_ref, kn_ref, vn_ref, ck_hbm, cv_hbm, o_ref,
                        kbuf, vbuf, ksem, vsem, *, n_pages, n_seq):
    b = pl.program_id(0)
    half = b % 2

    def seq_copies(seq, dst_half):
        copies = []
        for h in range(N_HEADS):
            for r in range(MOBA_TOPK):
                blk = sel_ref[(seq * N_HEADS + h) * MOBA_TOPK + r]
                for pg in range(PAGES_PER_BLOCK):
                    page = pt_ref[seq * n_pages + blk * PAGES_PER_BLOCK + pg]
                    src = pl.ds(pl.multiple_of(page * ATTN_WIDTH + h * HEAD_DIM, HEAD_DIM), HEAD_DIM)
                    dst = pl.ds((r * PAGES_PER_BLOCK + pg) * LANES, LANES)
                    copies.append(pltpu.make_async_copy(ck_hbm.at[src, :], kbuf.at[dst_half, h, :, dst],
                                                        ksem.at[dst_half]))
                    copies.append(pltpu.make_async_copy(cv_hbm.at[src, :], vbuf.at[dst_half, h, :, dst],
                                                        vsem.at[dst_half]))
        return copies

    @pl.when(b == 0)
    def _():
        for cp in seq_copies(0, 0):
            cp.start()

    @pl.when(b + 1 < n_seq)
    def _():
        for cp in seq_copies(b + 1, 1 - half):
            cp.start()

    for cp in seq_copies(b, half):
        cp.wait()

    q8 = q_ref[0]
    kn = kn_ref[0]
    vn = vn_ref[0]
    nt = (((1,), (1,)), ((), ()))
    l_past = jnp.concatenate(
        [jnp.dot(q8, kbuf[half, h], precision=HIGHEST, preferred_element_type=F32)[h:h + 1, :]
         for h in range(N_HEADS)], axis=0) * SM_SCALE
    l_self = jnp.sum(q8 * kn, axis=1, keepdims=True) * SM_SCALE
    m = jnp.maximum(jnp.max(l_past, axis=1, keepdims=True), l_self)
    p = jnp.exp(l_past - m)
    p_self = jnp.exp(l_self - m)
    den = jnp.sum(p, axis=1, keepdims=True) + p_self
    pv = jnp.concatenate(
        [lax.dot_general(p, vbuf[half, h], nt, precision=HIGHEST, preferred_element_type=F32)[h:h + 1, :]
         for h in range(N_HEADS)], axis=0)
    o_ref[0] = (p_self * vn + pv) / den


def _moba_sample(page_table_flat, sel_flat, q3, k3, v3, cache_kt, cache_vt, n_seq, n_pages):
    kern = functools.partial(_moba_sample_kernel, n_pages=n_pages, n_seq=n_seq)
    n_keys = MOBA_TOPK * MOBA_BLOCK
    tok = pl.BlockSpec((1, N_HEADS, HEAD_DIM), lambda b, pt, sel: (b, 0, 0))
    return pl.pallas_call(
        kern,
        grid_spec=pltpu.PrefetchScalarGridSpec(
            num_scalar_prefetch=2,
            grid=(n_seq,),
            in_specs=[tok, tok, tok, pl.BlockSpec(memory_space=pl.ANY), pl.BlockSpec(memory_space=pl.ANY)],
            out_specs=tok,
            scratch_shapes=[
                pltpu.VMEM((2, N_HEADS, HEAD_DIM, n_keys), F32),
                pltpu.VMEM((2, N_HEADS, HEAD_DIM, n_keys), F32),
                pltpu.SemaphoreType.DMA((2,)),
                pltpu.SemaphoreType.DMA((2,)),
            ],
        ),
        out_shape=jax.ShapeDtypeStruct((n_seq, N_HEADS, HEAD_DIM), F32),
        compiler_params=_cparams("arbitrary"),
        name="moba_sample",
    )(page_table_flat, sel_flat, q3, k3, v3, cache_kt, cache_vt)


def _token_tile(t, want):
    return want if t % want == 0 else t


def kernel(x_prompt, x_sample, cache_k, cache_v, state_conv, page_table, norm1_g, w_in, b_in, conv_w, conv_b,
           conv_ln_g, conv_ln_b, w_out, norm2_g, w_group, b_group, w_router, b_router, w_gate, w_up, w_down,
           norm_f_g):
    bsz, slen, _ = x_prompt.shape
    dbsz, dlen, _ = x_sample.shape
    depth = w_in.shape[0]
    assert depth == 1 and dlen == 1, "one layer and one new sample token per sequence"
    n_pages, page_rows = page_table.shape[1], cache_k.shape[2]
    assert MOBA_BLOCK == PAGES_PER_BLOCK * page_rows and slen % (MOBA_BLOCK * PAST_SPAN) == 0
    assert page_rows == LANES and MOBA_TOPK <= n_pages // PAGES_PER_BLOCK <= LANES and n_pages % KMEAN_RING == 0
    past_len = n_pages * page_rows
    l = 0

    w_in_b = w_in[l].astype(BF16)
    w_out_b = w_out[l].astype(BF16)
    wg_b, wu_b, wd_b = w_gate[l].astype(BF16), w_up[l].astype(BF16), w_down[l].astype(BF16)
    pad_cols = ROUTER_COLS - N_EXPERTS - N_EXPERT_GROUPS
    w_rt = jnp.pad(jnp.concatenate([w_router[l], w_group[l]], axis=1), ((0, 0), (0, pad_cols)))
    b_rt = jnp.pad(jnp.concatenate([b_router[l], b_group[l]])[None, :], ((0, 0), (0, pad_cols)))
    g1, g2, gf = norm1_g[l][None, :], norm2_g[l][None, :], norm_f_g[None, :]
    b_in2, cb = b_in[l][None, :], conv_b[l][None, :]
    lg, lb = conv_ln_g[l][None, :], conv_ln_b[l][None, :]

    t_p = bsz * slen
    xp = x_prompt.reshape(t_p, D_MODEL)
    tm_in = _token_tile(slen, 512)
    q_p, kt_p, vt_p, u_p, kb_p, vtb_p, km_p = _in_projection(xp, g1, w_in_b, b_in2, tm_in, True, 0, 1, slen)
    km_p = km_p.reshape(bsz, slen // MOBA_BLOCK, ATTN_WIDTH)
    attn_p = _moba_prompt(q_p, kb_p, vtb_p, km_p, bsz, slen)
    conv_p = _conv_prompt(u_p, conv_w[l], cb, lg, lb, bsz, slen, _token_tile(slen, 256))
    y_p = _post_mixer_grouped(xp, attn_p, conv_p, w_out_b, g2, w_rt, b_rt, wg_b, wu_b, wd_b, gf)

    xs = x_sample.reshape(dbsz, D_MODEL)
    q_s, k_s, v_s, u_s = _in_projection(xs, g1, w_in_b, b_in2, dbsz, False, past_len, 0, dbsz)
    heads = lambda t: t.reshape(dbsz, N_HEADS, HEAD_DIM)
    ck2 = cache_k[l].transpose(0, 2, 3, 1).reshape(-1, page_rows)
    cv2 = cache_v[l].transpose(0, 2, 3, 1).reshape(-1, page_rows)
    pt_flat = page_table.reshape(-1)
    q_lanes = jnp.broadcast_to(q_s[:, :, None], (dbsz, ATTN_WIDTH, LANES))
    sel = _kmean_select(pt_flat, q_lanes, ck2, dbsz, n_pages)
    sel_flat = sel[:, :, :MOBA_TOPK].reshape(-1)
    attn_s = _moba_sample(pt_flat, sel_flat, heads(q_s), heads(k_s), heads(v_s), ck2, cv2, dbsz, n_pages)
    conv_s = _conv_sample(state_conv[l], u_s, conv_w[l], cb, lg, lb)
    y_s = _post_mixer(xs, attn_s.reshape(dbsz, ATTN_WIDTH).astype(BF16), conv_s, w_out_b, g2, w_rt, b_rt,
                      wg_b, wu_b, wd_b, gf, dbsz, "post_mixer_sample")

    hist = CONV_LEN - 1
    kv_p = lambda t: t.reshape(bsz, N_HEADS, HEAD_DIM, slen).transpose(0, 3, 1, 2)[None]
    kv_s = lambda t: t.reshape(1, dbsz, 1, N_HEADS, HEAD_DIM)
    new_conv_p = u_p.reshape(bsz, slen, CONV_CH)[:, slen - hist:][None]
    new_conv_s = jnp.concatenate([state_conv[l][:, 1:], u_s[:, None, :]], axis=1)[None]
    return (y_p.reshape(bsz, slen, D_MODEL), y_s.reshape(dbsz, 1, D_MODEL), kv_p(kt_p), kv_p(vt_p), new_conv_p,
            kv_s(k_s), kv_s(v_s), new_conv_s)
```

```python
import functools
import math

import jax
import jax.numpy as jnp
import numpy as np
from jax import lax
from jax.experimental import pallas as pl
from jax.experimental.pallas import tpu as pltpu

F32 = jnp.float32
BF16 = jnp.bfloat16
HIGHEST = lax.Precision.HIGHEST

D_MODEL = 1024
ATTN_WIDTH = 512
CONV_CH = 512
HEAD_DIM = 64
N_HEADS = 8
IN_WIDTH = 3 * ATTN_WIDTH + 2 * CONV_CH
CONV_LEN = 31
MOBA_BLOCK = 256
MOBA_TOPK = 3
ROPE_THETA = 10000.0
N_EXPERT_GROUPS = 4
EXPERTS_PER_GROUP = 4
N_EXPERTS = 16
D_EXPERT = 512
RMS_EPS = 1e-6
LN_EPS = 1e-5
NEG_INF = -1e30
SM_SCALE = HEAD_DIM ** -0.5
_ROPE_LOG_STEP = -math.log(ROPE_THETA) / (HEAD_DIM // 2)
ROPE_LOG_STEP_HI = float(np.float32(_ROPE_LOG_STEP))
ROPE_LOG_STEP_LO = _ROPE_LOG_STEP - ROPE_LOG_STEP_HI

LANES = 128
SUBLANES = 8
HEADS_PER_LANE_TILE = LANES // HEAD_DIM
ROUTER_COLS = LANES
GROUP_COL0 = N_EXPERTS
VMEM_LIMIT = 56 * 1024 * 1024
CONV_HALO = 32
CONV_CHUNK = 64
LOG2_E = math.log2(math.e)
PAST_SPAN = 2
ATTN_TILES = 4
KMEAN_RING = 16
PAGES_PER_BLOCK = 2
MOE_TILE = 512
TOKEN_TILE = 512
GATHER_UNROLL = 8
ROW_WIDTH = D_MODEL + ROUTER_COLS
GROUP_ID_COL = ROUTER_COLS - 1


def _cparams(*sem):
    return pltpu.CompilerParams(dimension_semantics=sem, vmem_limit_bytes=VMEM_LIMIT)


def _inproj_kernel(x_ref, g_ref, w_ref, b_ref, q_ref, k_ref, v_ref, u_ref, *prompt_refs,
                   pos0, pos_stride, n_pos_tiles):
    x = x_ref[...]
    tm = x.shape[0]
    h = x * lax.rsqrt(jnp.mean(x * x, axis=-1, keepdims=True) + RMS_EPS) * g_ref[...]
    z = jnp.dot(h.astype(BF16), w_ref[...], preferred_element_type=F32) + b_ref[...]
    a = ATTN_WIDTH
    reps = a // LANES
    half = HEAD_DIM // 2
    lane_t = lax.broadcasted_iota(jnp.int32, (1, LANES), 1)
    freq_ix = (lane_t % half).astype(F32)
    inv_freq = jnp.exp(freq_ix * ROPE_LOG_STEP_HI) * jnp.exp(freq_ix * ROPE_LOG_STEP_LO)
    tile_pos = pos0 + (pl.program_id(0) % n_pos_tiles) * (tm * pos_stride)
    pos = tile_pos + lax.broadcasted_iota(jnp.int32, (tm, 1), 0) * pos_stride
    ang = pos.astype(F32) * inv_freq
    sin_sign = jnp.where((lane_t % HEAD_DIM) < half, -1.0, 1.0)
    cos = jnp.concatenate([jnp.cos(ang)] * reps, axis=-1)
    sin = jnp.concatenate([jnp.sin(ang) * sin_sign] * reps, axis=-1)
    lane = lax.broadcasted_iota(jnp.int32, (1, a), 1)
    first_half = (lane % HEAD_DIM) < half

    def rope(t):
        from_below = pltpu.roll(t, HEAD_DIM // 2, axis=1)
        from_above = pltpu.roll(t, a - HEAD_DIM // 2, axis=1)
        return t * cos + jnp.where(first_half, from_above, from_below) * sin

    q = rope(z[:, 0:a])
    k = rope(z[:, a:2 * a])
    v = z[:, 2 * a:3 * a]
    u = z[:, 3 * a:3 * a + CONV_CH] * jax.nn.sigmoid(z[:, 3 * a + CONV_CH:])
    q_ref[...] = q
    u_ref[...] = u
    if prompt_refs:
        kb_ref, vtb_ref, km_ref = prompt_refs
        v_t = v.T
        k_ref[...] = k.T
        v_ref[...] = v_t
        kb_ref[...] = k.astype(BF16)
        vtb_ref[...] = v_t.astype(BF16)
        nblk = k.shape[0] // MOBA_BLOCK
        km_ref[0] = jnp.mean(k.reshape(nblk, MOBA_BLOCK, a), axis=1)
    else:
        k_ref[...] = k
        v_ref[...] = v


def _in_projection(x2d, norm_g, w_bf16, b_in, tm, prompt, pos0, pos_stride, seq_len):
    t = x2d.shape[0]
    n_t = t // tm
    kern = functools.partial(_inproj_kernel, pos0=pos0, pos_stride=pos_stride, n_pos_tiles=seq_len // tm)
    row = lambda i: (i, 0)
    fixed = lambda i: (0, 0)
    n_pos = seq_len // tm
    wide = jax.ShapeDtypeStruct((t, ATTN_WIDTH), F32)
    wide_spec = pl.BlockSpec((tm, ATTN_WIDTH), row)
    if prompt:
        nblk = tm // MOBA_BLOCK
        n_seq = t // seq_len
        feat_spec = pl.BlockSpec((ATTN_WIDTH, tm), lambda i: (i // n_pos, i % n_pos))
        feat_f32 = jax.ShapeDtypeStruct((n_seq * ATTN_WIDTH, seq_len), F32)
        feat_bf16 = jax.ShapeDtypeStruct((n_seq * ATTN_WIDTH, seq_len), BF16)
        out_shape = [wide, feat_f32, feat_f32, wide, jax.ShapeDtypeStruct((t, ATTN_WIDTH), BF16), feat_bf16,
                     jax.ShapeDtypeStruct((n_t, nblk, ATTN_WIDTH), F32)]
        out_specs = [wide_spec, feat_spec, feat_spec, wide_spec, wide_spec, feat_spec,
                     pl.BlockSpec((1, nblk, ATTN_WIDTH), lambda i: (i, 0, 0))]
    else:
        out_shape = [wide] * 4
        out_specs = [wide_spec] * 4
    return pl.pallas_call(
        kern,
        grid=(n_t,),
        in_specs=[
            pl.BlockSpec((tm, D_MODEL), row),
            pl.BlockSpec((1, D_MODEL), fixed),
            pl.BlockSpec((D_MODEL, IN_WIDTH), fixed),
            pl.BlockSpec((1, IN_WIDTH), fixed),
        ],
        out_specs=out_specs,
        out_shape=out_shape,
        compiler_params=_cparams("parallel"),
        name="in_projection_prompt" if prompt else "in_projection_sample",
    )(x2d, norm_g, w_bf16, b_in)


def _moba_prompt_kernel(q_ref, kb_ref, vt_ref, km_ref, o_ref, m_sc, l_sc, acc_sc, bias_sc, s_even, s_odd):
    qi = pl.program_id(2)
    blk_rows = MOBA_BLOCK
    span_rows = PAST_SPAN * blk_rows
    tiles = range(ATTN_TILES)
    cols = lambda t: slice(t * LANES, (t + 1) * LANES)
    nt = (((1,), (1,)), ((), ()))
    lane = lax.broadcasted_iota(jnp.int32, (1, LANES), 1)
    head0 = lane < HEAD_DIM
    n_blocks = km_ref.shape[1]

    qb = []
    for t in tiles:
        q = q_ref[:, cols(t)]
        q2 = jnp.concatenate([jnp.where(head0, q, 0.0), jnp.where(head0, 0.0, q)], axis=0)
        km = km_ref[0, :, cols(t)]
        km_hi, q_hi = km.astype(BF16), q2.astype(BF16)
        km_lo, q_lo = (km - km_hi.astype(F32)).astype(BF16), (q2 - q_hi.astype(F32)).astype(BF16)
        gate = (lax.dot_general(km_hi, q_hi, nt, preferred_element_type=F32)
                + lax.dot_general(km_lo, q_hi, nt, preferred_element_type=F32)
                + lax.dot_general(km_hi, q_lo, nt, preferred_element_type=F32))
        blk = lax.broadcasted_iota(jnp.int32, gate.shape, 0)
        fully_past = blk < qi
        gate = jnp.where(fully_past, gate, NEG_INF)
        rank = jnp.zeros(gate.shape, jnp.int32)
        for m in range(n_blocks):
            gm = gate[m:m + 1, :]
            ahead = (gm > gate) | ((gm == gate) & (m < blk))
            rank = rank + ahead.astype(jnp.int32)
        bias_sc[t] = jnp.where(fully_past & (rank < MOBA_TOPK), 0.0, NEG_INF)
        qb.append((q2 * (SM_SCALE * LOG2_E)).astype(BF16))

    def scores(t, j0):
        r0 = pl.multiple_of(j0 * blk_rows, blk_rows)
        s = lax.dot_general(kb_ref[pl.ds(r0, span_rows), cols(t)], qb[t], nt, preferred_element_type=F32)
        return jnp.concatenate([s[i * blk_rows:(i + 1) * blk_rows] + bias_sc[t, pl.ds(j0 + i, 1), :]
                                for i in range(PAST_SPAN)], axis=0)

    def fold(t, s, j0):
        r0 = pl.multiple_of(j0 * blk_rows, blk_rows)
        m_old = m_sc[t]
        m_new = jnp.maximum(m_old, jnp.max(s, axis=0, keepdims=True))
        alpha = jnp.exp2(m_old - m_new)
        p = jnp.exp2(s - m_new)
        l_sc[t] = alpha * l_sc[t] + jnp.sum(p, axis=0, keepdims=True)
        acc_sc[t] = alpha * acc_sc[t] + jnp.dot(vt_ref[cols(t), pl.ds(r0, span_rows)], p.astype(BF16),
                                                preferred_element_type=F32)
        m_sc[t] = m_new

    n_spans = (qi + PAST_SPAN - 1) // PAST_SPAN
    last_span = jnp.maximum(n_spans - 1, 0)
    span0 = lambda i: jnp.minimum(i, last_span) * PAST_SPAN
    for t in tiles:
        s_even[t] = scores(t, 0)

    own0 = pl.multiple_of(qi * blk_rows, blk_rows)
    for t in tiles:
        s = lax.dot_general(kb_ref[pl.ds(own0, blk_rows), cols(t)], qb[t], nt, preferred_element_type=F32)
        k_ix = lax.broadcasted_iota(jnp.int32, s.shape, 0)
        q_ix = lax.broadcasted_iota(jnp.int32, s.shape, 1) % blk_rows
        s = jnp.where(k_ix <= q_ix, s, NEG_INF)
        m0 = jnp.max(s, axis=0, keepdims=True)
        p = jnp.exp2(s - m0)
        m_sc[t] = m0
        l_sc[t] = jnp.sum(p, axis=0, keepdims=True)
        acc_sc[t] = jnp.dot(vt_ref[cols(t), pl.ds(own0, blk_rows)], p.astype(BF16), preferred_element_type=F32)

    def pair_body(u, carry):
        i = 2 * u
        for t in tiles:
            s_odd[t] = scores(t, span0(i + 1))
        for t in tiles:
            fold(t, s_even[t], i * PAST_SPAN)
        for t in tiles:
            s_even[t] = scores(t, span0(i + 2))
        for t in tiles:
            fold(t, s_odd[t], (i + 1) * PAST_SPAN)
        return carry

    lax.fori_loop(0, n_spans // 2, pair_body, 0)

    @pl.when(n_spans % 2 == 1)
    def _():
        for t in tiles:
            fold(t, s_even[t], (n_spans - 1) * PAST_SPAN)

    for t in tiles:
        out = acc_sc[t] / l_sc[t]
        out_t = jnp.concatenate([out[:HEAD_DIM, :blk_rows], out[HEAD_DIM:, blk_rows:]], axis=0)
        o_ref[:, cols(t)] = out_t.T.astype(o_ref.dtype)


def _moba_prompt(q, kb, vtb, kmean, bsz, slen):
    n_q = slen // MOBA_BLOCK
    width = ATTN_TILES * LANES
    n_w = ATTN_WIDTH // width
    cols2 = HEADS_PER_LANE_TILE * MOBA_BLOCK
    return pl.pallas_call(
        _moba_prompt_kernel,
        grid=(bsz, n_w, n_q),
        in_specs=[
            pl.BlockSpec((MOBA_BLOCK, width), lambda b, w, qi: (b * n_q + qi, w)),
            pl.BlockSpec((slen, width), lambda b, w, qi: (b, w)),
            pl.BlockSpec((width, slen), lambda b, w, qi: (b * n_w + w, 0)),
            pl.BlockSpec((1, n_q, width), lambda b, w, qi: (b, 0, w)),
        ],
        out_specs=pl.BlockSpec((MOBA_BLOCK, width), lambda b, w, qi: (b * n_q + qi, w)),
        out_shape=jax.ShapeDtypeStruct((bsz * slen, ATTN_WIDTH), BF16),
        scratch_shapes=[
            pltpu.VMEM((ATTN_TILES, 1, cols2), F32),
            pltpu.VMEM((ATTN_TILES, 1, cols2), F32),
            pltpu.VMEM((ATTN_TILES, LANES, cols2), F32),
            pltpu.VMEM((ATTN_TILES, n_q, cols2), F32),
            pltpu.VMEM((ATTN_TILES, PAST_SPAN * MOBA_BLOCK, cols2), F32),
            pltpu.VMEM((ATTN_TILES, PAST_SPAN * MOBA_BLOCK, cols2), F32),
        ],
        compiler_params=_cparams("parallel", "parallel", "arbitrary"),
        name="moba_prompt",
    )(q, kb, vtb, kmean)


def _ln_swish(c, g, b):
    mu = jnp.mean(c, axis=-1, keepdims=True)
    d = c - mu
    var = jnp.mean(d * d, axis=-1, keepdims=True)
    y = d * lax.rsqrt(var + LN_EPS) * g + b
    return y * jax.nn.sigmoid(y)


def _conv_tile(first_tile, prev_ref, cur_ref, w_ref, cb_ref, g_ref, b_ref, ext_sc, store):
    tc = cur_ref.shape[0]
    ext_sc[0:CONV_HALO, :] = jnp.where(first_tile, 0.0, prev_ref[...])
    ext_sc[CONV_HALO:CONV_HALO + tc, :] = cur_ref[...]
    ext_sc[CONV_HALO + tc:CONV_HALO + tc + SUBLANES, :] = jnp.zeros((SUBLANES, CONV_CH), F32)
    lead = CONV_HALO - (CONV_LEN - 1)
    for r0 in range(0, tc, CONV_CHUNK):
        acc = jnp.zeros((CONV_CHUNK, CONV_CH), F32)
        for res in range(SUBLANES):
            taps = [j for j in range(CONV_LEN) if (j + lead) % SUBLANES == res]
            part = None
            for j in taps:
                base = r0 + j + lead - res
                term = w_ref[j:j + 1, :] * ext_sc[base:base + CONV_CHUNK + SUBLANES, :]
                part = term if part is None else part + term
            if part is not None:
                acc = acc + part[res:res + CONV_CHUNK]
        store(r0, _ln_swish(acc + cb_ref[...], g_ref[...], b_ref[...]))


def _conv_sample_kernel(st_ref, u_ref, w_ref, cb_ref, g_ref, b_ref, o_ref):
    hist = CONV_LEN - 1
    for b in range(st_ref.shape[0]):
        u_row = u_ref[b:b + 1, :]
        c = jnp.sum(st_ref[b] * w_ref[0:hist, :], axis=0, keepdims=True) + w_ref[hist:hist + 1, :] * u_row
        y = _ln_swish(c + cb_ref[...], g_ref[...], b_ref[...])
        o_ref[b:b + 1, :] = y.astype(o_ref.dtype)


def _conv_sample(state, u2d, conv_w, conv_b, ln_g, ln_b):
    n = u2d.shape[0]
    return pl.pallas_call(
        _conv_sample_kernel,
        out_shape=jax.ShapeDtypeStruct((n, CONV_CH), BF16),
        compiler_params=pltpu.CompilerParams(vmem_limit_bytes=VMEM_LIMIT),
        name="conv_sample",
    )(state, u2d, conv_w, conv_b, ln_g, ln_b)


def _route(logits, with_group=False):
    c = lax.broadcasted_iota(jnp.int32, logits.shape, 1)
    big = jnp.int32(ROUTER_COLS)
    is_group = (c >= GROUP_COL0) & (c < GROUP_COL0 + N_EXPERT_GROUPS)
    gl = jnp.where(is_group, logits, NEG_INF)
    gmax = jnp.max(gl, axis=1, keepdims=True)
    g_sel = jnp.min(jnp.where(gl == gmax, c - GROUP_COL0, big), axis=1, keepdims=True)
    g_w = 1.0 / jnp.sum(jnp.exp(gl - gmax), axis=1, keepdims=True)
    in_group = (c < N_EXPERTS) & ((c // EXPERTS_PER_GROUP) == g_sel)
    el = jnp.where(in_group, logits, NEG_INF)
    m1 = jnp.max(el, axis=1, keepdims=True)
    i1 = jnp.min(jnp.where(el == m1, c, big), axis=1, keepdims=True)
    el2 = jnp.where(c == i1, NEG_INF, el)
    m2 = jnp.max(el2, axis=1, keepdims=True)
    i2 = jnp.min(jnp.where(el2 == m2, c, big), axis=1, keepdims=True)
    e21 = jnp.exp(m2 - m1)
    inv = 1.0 / (1.0 + e21)
    comb = g_w * jnp.where(c == i1, inv, jnp.where(c == i2, e21 * inv, 0.0))
    return (comb, g_sel) if with_group else comb


def _post_kernel(x_ref, a_ref, c_ref, wo_ref, g2_ref, wr_ref, br_ref, wg_ref, wu_ref, wd_ref, gf_ref,
                 y_ref, x1_sc, h2_sc, comb_sc, acc_sc):
    e = pl.program_id(1)

    @pl.when(e == 0)
    def _():
        x1 = (x_ref[...]
              + jnp.dot(a_ref[...], wo_ref[0:ATTN_WIDTH, :], preferred_element_type=F32)
              + jnp.dot(c_ref[...], wo_ref[ATTN_WIDTH:D_MODEL, :], preferred_element_type=F32))
        x1_sc[...] = x1
        h2 = x1 * lax.rsqrt(jnp.mean(x1 * x1, axis=-1, keepdims=True) + RMS_EPS) * g2_ref[...]
        h2_sc[...] = h2.astype(BF16)
        logits = jnp.dot(h2, wr_ref[...], precision=HIGHEST, preferred_element_type=F32) + br_ref[...]
        comb_sc[...] = _route(logits)
        acc_sc[...] = jnp.zeros_like(acc_sc)

    h2b = h2_sc[...]
    col = lax.broadcasted_iota(jnp.int32, comb_sc.shape, 1)
    ce = jnp.sum(jnp.where(col == e, comb_sc[...], 0.0), axis=1, keepdims=True)
    gt = jnp.dot(h2b, wg_ref[0].astype(BF16), preferred_element_type=F32)
    up = jnp.dot(h2b, wu_ref[0].astype(BF16), preferred_element_type=F32)
    hid = gt * jax.nn.sigmoid(gt) * up * ce
    acc_sc[...] += jnp.dot(hid.astype(BF16), wd_ref[0].astype(BF16), preferred_element_type=F32)

    @pl.when(e == N_EXPERTS - 1)
    def _():
        x2 = x1_sc[...] + acc_sc[...]
        y_ref[...] = x2 * lax.rsqrt(jnp.mean(x2 * x2, axis=-1, keepdims=True) + RMS_EPS) * gf_ref[...]


def _post_mixer(x2d, attn, conv, w_out, g2, w_rt, b_rt, wg, wu, wd, gf, tm, name):
    t = x2d.shape[0]
    row = lambda i, e: (i, 0)
    fixed = lambda i, e: (0, 0)
    return pl.pallas_call(
        _post_kernel,
        grid=(t // tm, N_EXPERTS),
        in_specs=[
            pl.BlockSpec((tm, D_MODEL), row),
            pl.BlockSpec((tm, ATTN_WIDTH), row),
            pl.BlockSpec((tm, CONV_CH), row),
            pl.BlockSpec((D_MODEL, D_MODEL), fixed),
            pl.BlockSpec((1, D_MODEL), fixed),
            pl.BlockSpec((D_MODEL, ROUTER_COLS), fixed),
            pl.BlockSpec((1, ROUTER_COLS), fixed),
            pl.BlockSpec((1, D_MODEL, D_EXPERT), lambda i, e: (e, 0, 0)),
            pl.BlockSpec((1, D_MODEL, D_EXPERT), lambda i, e: (e, 0, 0)),
            pl.BlockSpec((1, D_EXPERT, D_MODEL), lambda i, e: (e, 0, 0)),
            pl.BlockSpec((1, D_MODEL), fixed),
        ],
        out_specs=pl.BlockSpec((tm, D_MODEL), row),
        out_shape=jax.ShapeDtypeStruct((t, D_MODEL), F32),
        scratch_shapes=[
            pltpu.VMEM((tm, D_MODEL), F32),
            pltpu.VMEM((tm, D_MODEL), BF16),
            pltpu.VMEM((tm, ROUTER_COLS), F32),
            pltpu.VMEM((tm, D_MODEL), F32),
        ],
        compiler_params=_cparams("parallel", "arbitrary"),
        name=name,
    )(x2d, attn, conv, w_out, g2, w_rt, b_rt, wg, wu, wd, gf)


def _mix_route_kernel(x_ref, a_ref, uprev_ref, u_ref, cw_ref, cb_ref, lg_ref, lb_ref, wo_ref, g2_ref, wr_ref, br_ref,
                      x1_ref, row_ref, ext_sc, conv_sc, *, tiles_per_seq):
    def store(r0, y):
        conv_sc[r0:r0 + CONV_CHUNK, :] = y.astype(BF16)

    first_tile = pl.program_id(0) % tiles_per_seq == 0
    _conv_tile(first_tile, uprev_ref, u_ref, cw_ref, cb_ref, lg_ref, lb_ref, ext_sc, store)
    x1 = (x_ref[...]
          + jnp.dot(a_ref[...], wo_ref[0:ATTN_WIDTH, :], preferred_element_type=F32)
          + jnp.dot(conv_sc[...], wo_ref[ATTN_WIDTH:D_MODEL, :], preferred_element_type=F32))
    x1_ref[...] = x1
    h2 = x1 * lax.rsqrt(jnp.mean(x1 * x1, axis=-1, keepdims=True) + RMS_EPS) * g2_ref[...]
    h_hi = h2.astype(BF16)
    h_lo = (h2 - h_hi.astype(F32)).astype(BF16)
    w = wr_ref[...]
    w_hi = w.astype(BF16)
    w_lo = (w - w_hi.astype(F32)).astype(BF16)
    logits = (jnp.dot(h_hi, w_hi, preferred_element_type=F32) + jnp.dot(h_lo, w_hi, preferred_element_type=F32)
              + jnp.dot(h_hi, w_lo, preferred_element_type=F32) + br_ref[...])
    comb, g_sel = _route(logits, with_group=True)
    col = lax.broadcasted_iota(jnp.int32, comb.shape, 1)
    row_ref[:, 0:D_MODEL] = h2
    row_ref[:, D_MODEL:ROW_WIDTH] = jnp.where(col == GROUP_ID_COL, g_sel.astype(F32), comb)


def _mix_route(x2d, attn, u2d, conv_w, conv_b, ln_g, ln_b, w_out, g2, w_rt, b_rt, tm, slen):
    t = x2d.shape[0]
    tiles_per_seq = slen // tm
    halo_per_tile = tm // CONV_HALO
    row = lambda i: (i, 0)
    fixed = lambda i: (0, 0)
    kern = functools.partial(_mix_route_kernel, tiles_per_seq=tiles_per_seq)
    return pl.pallas_call(
        kern,
        grid=(t // tm,),
        in_specs=[
            pl.BlockSpec((tm, D_MODEL), row),
            pl.BlockSpec((tm, ATTN_WIDTH), row),
            pl.BlockSpec((CONV_HALO, CONV_CH), lambda i: (jnp.maximum(i * halo_per_tile - 1, 0), 0)),
            pl.BlockSpec((tm, CONV_CH), row),
            pl.BlockSpec((CONV_LEN, CONV_CH), fixed),
            pl.BlockSpec((1, CONV_CH), fixed),
            pl.BlockSpec((1, CONV_CH), fixed),
            pl.BlockSpec((1, CONV_CH), fixed),
            pl.BlockSpec((D_MODEL, D_MODEL), fixed),
            pl.BlockSpec((1, D_MODEL), fixed),
            pl.BlockSpec((D_MODEL, ROUTER_COLS), fixed),
            pl.BlockSpec((1, ROUTER_COLS), fixed),
        ],
        out_specs=[pl.BlockSpec((tm, D_MODEL), row), pl.BlockSpec((tm, ROW_WIDTH), row)],
        out_shape=[jax.ShapeDtypeStruct((t, D_MODEL), F32), jax.ShapeDtypeStruct((t, ROW_WIDTH), F32)],
        scratch_shapes=[pltpu.VMEM((CONV_HALO + tm + SUBLANES, CONV_CH), F32), pltpu.VMEM((tm, CONV_CH), BF16)],
        compiler_params=_cparams("parallel"),
        name="conv_mix_route_prompt",
    )(x2d, attn, u2d, u2d, conv_w, conv_b, ln_g, ln_b, w_out, g2, w_rt, b_rt)


def _gather_rows(idx_ref, base, src_hbm, dst, sem, n_rows):
    def body(c, carry):
        for k in range(GATHER_UNROLL):
            r = c * GATHER_UNROLL + k
            pltpu.make_async_copy(src_hbm.at[pl.ds(idx_ref[base + r], 1), :], dst.at[pl.ds(r, 1), :], sem).start()
        return carry
    lax.fori_loop(0, n_rows // GATHER_UNROLL, body, 0)


def _wait_rows(src_hbm, dst, sem, n_rows):
    def body(c, carry):
        for k in range(GATHER_UNROLL):
            r = c * GATHER_UNROLL + k
            pltpu.make_async_copy(src_hbm.at[pl.ds(0, 1), :], dst.at[pl.ds(r, 1), :], sem).wait()
        return carry
    lax.fori_loop(0, n_rows // GATHER_UNROLL, body, 0)


def _moe_grouped_kernel(tg_ref, src_ref, rows_hbm, wg_ref, wu_ref, wd_ref, y_ref, xbuf, sem, hb_sc, acc_sc,
                        *, n_tiles):
    i = pl.program_id(0)
    j = pl.program_id(1)
    slot = i % 2
    group = tg_ref[i]

    @pl.when(j == 0)
    def _():
        @pl.when(i == 0)
        def _():
            _gather_rows(src_ref, 0, rows_hbm, xbuf.at[0], sem.at[0], MOE_TILE)

        @pl.when(i + 1 < n_tiles)
        def _():
            _gather_rows(src_ref, (i + 1) * MOE_TILE, rows_hbm, xbuf.at[1 - slot], sem.at[1 - slot], MOE_TILE)

        _wait_rows(rows_hbm, xbuf.at[slot], sem.at[slot], MOE_TILE)
        hb_sc[...] = xbuf[slot, :, 0:D_MODEL].astype(BF16)
        acc_sc[...] = jnp.zeros_like(acc_sc)

    @pl.when(group < N_EXPERT_GROUPS)
    def _():
        e = group * EXPERTS_PER_GROUP + j
        comb = xbuf[slot, :, D_MODEL:ROW_WIDTH]
        col = lax.broadcasted_iota(jnp.int32, comb.shape, 1)
        ce = jnp.sum(jnp.where(col == e, comb, 0.0), axis=1, keepdims=True)
        hb = hb_sc[...]
        gt = jnp.dot(hb, wg_ref[0].astype(BF16), preferred_element_type=F32)
        up = jnp.dot(hb, wu_ref[0].astype(BF16), preferred_element_type=F32)
        hid = gt * jax.nn.sigmoid(gt) * up * ce
        acc_sc[...] += jnp.dot(hid.astype(BF16), wd_ref[0].astype(BF16), preferred_element_type=F32)

    @pl.when(j == EXPERTS_PER_GROUP - 1)
    def _():
        y_ref[...] = acc_sc[...]


def _moe_grouped(tile_group, src_of_sorted, rows, wg, wu, wd, n_tiles):
    kern = functools.partial(_moe_grouped_kernel, n_tiles=n_tiles)
    last = N_EXPERT_GROUPS - 1
    w_ix = lambda i, j, tg, src: (jnp.minimum(tg[i], last) * EXPERTS_PER_GROUP + j, 0, 0)
    return pl.pallas_call(
        kern,
        grid_spec=pltpu.PrefetchScalarGridSpec(
            num_scalar_prefetch=2,
            grid=(n_tiles, EXPERTS_PER_GROUP),
            in_specs=[
                pl.BlockSpec(memory_space=pl.ANY),
                pl.BlockSpec((1, D_MODEL, D_EXPERT), w_ix),
                pl.BlockSpec((1, D_MODEL, D_EXPERT), w_ix),
                pl.BlockSpec((1, D_EXPERT, D_MODEL), w_ix),
            ],
            out_specs=pl.BlockSpec((MOE_TILE, D_MODEL), lambda i, j, tg, src: (i, 0)),
            scratch_shapes=[
                pltpu.VMEM((2, MOE_TILE, ROW_WIDTH), F32),
                pltpu.SemaphoreType.DMA((2,)),
                pltpu.VMEM((MOE_TILE, D_MODEL), BF16),
                pltpu.VMEM((MOE_TILE, D_MODEL), F32),
            ],
        ),
        out_shape=jax.ShapeDtypeStruct((n_tiles * MOE_TILE, D_MODEL), F32),
        compiler_params=_cparams("arbitrary", "arbitrary"),
        name="moe_grouped_prompt",
    )(tile_group, src_of_sorted, rows, wg, wu, wd)


def _moe_finish_kernel(pos_ref, x1_ref, gf_ref, ys_hbm, y_ref, buf, sem, *, n_tiles):
    i = pl.program_id(0)
    slot = i % 2
    tm = x1_ref.shape[0]

    @pl.when(i == 0)
    def _():
        _gather_rows(pos_ref, 0, ys_hbm, buf.at[0], sem.at[0], tm)

    @pl.when(i + 1 < n_tiles)
    def _():
        _gather_rows(pos_ref, (i + 1) * tm, ys_hbm, buf.at[1 - slot], sem.at[1 - slot], tm)

    _wait_rows(ys_hbm, buf.at[slot], sem.at[slot], tm)
    x2 = x1_ref[...] + buf[slot]
    y_ref[...] = x2 * lax.rsqrt(jnp.mean(x2 * x2, axis=-1, keepdims=True) + RMS_EPS) * gf_ref[...]


def _moe_finish(pos_of_token, x1, gf, ys, tm):
    t = x1.shape[0]
    n_tiles = t // tm
    kern = functools.partial(_moe_finish_kernel, n_tiles=n_tiles)
    return pl.pallas_call(
        kern,
        grid_spec=pltpu.PrefetchScalarGridSpec(
            num_scalar_prefetch=1,
            grid=(n_tiles,),
            in_specs=[
                pl.BlockSpec((tm, D_MODEL), lambda i, pos: (i, 0)),
                pl.BlockSpec((1, D_MODEL), lambda i, pos: (0, 0)),
                pl.BlockSpec(memory_space=pl.ANY),
            ],
            out_specs=pl.BlockSpec((tm, D_MODEL), lambda i, pos: (i, 0)),
            scratch_shapes=[pltpu.VMEM((2, tm, D_MODEL), F32), pltpu.SemaphoreType.DMA((2,))],
        ),
        out_shape=jax.ShapeDtypeStruct((t, D_MODEL), F32),
        compiler_params=_cparams("arbitrary"),
        name="moe_finish_prompt",
    )(pos_of_token, x1, gf, ys)


def _sort_plan(group_id, n_tiles):
    t = group_id.shape[0]
    onehot = (group_id[:, None] == jnp.arange(N_EXPERT_GROUPS, dtype=jnp.int32)[None, :]).astype(jnp.int32)
    counts = jnp.sum(onehot, axis=0)
    rank = jnp.sum((jnp.cumsum(onehot, axis=0) - onehot) * onehot, axis=1)
    padded = (counts + MOE_TILE - 1) // MOE_TILE * MOE_TILE
    group_end = jnp.cumsum(padded)
    group_start = group_end - padded
    pos = group_start[group_id] + rank
    src = jnp.zeros((n_tiles * MOE_TILE,), jnp.int32).at[pos].set(jnp.arange(t, dtype=jnp.int32))
    tile_start = jnp.arange(n_tiles, dtype=jnp.int32) * MOE_TILE
    tile_group = jnp.sum((tile_start[:, None] >= group_end[None, :]).astype(jnp.int32), axis=1)
    return pos.astype(jnp.int32), src, tile_group.astype(jnp.int32)


def _post_mixer_grouped(x2d, attn, u2d, conv_w, conv_b, ln_g, ln_b, w_out_b, g2, w_rt, b_rt, wg, wu, wd, gf, slen):
    t = x2d.shape[0]
    n_tiles = t // MOE_TILE + N_EXPERT_GROUPS
    x1, rows = _mix_route(x2d, attn, u2d, conv_w, conv_b, ln_g, ln_b, w_out_b, g2, w_rt, b_rt, TOKEN_TILE, slen)
    group_id = rows[:, D_MODEL + GROUP_ID_COL].astype(jnp.int32)
    pos, src, tile_group = _sort_plan(group_id, n_tiles)
    ys = _moe_grouped(tile_group, src, rows, wg, wu, wd, n_tiles)
    return _moe_finish(pos, x1, gf, ys, TOKEN_TILE)


def _kmean_select_kernel(pt_ref, q_ref, ck_hbm, sel_ref, buf, sem, sums_sc, *, n_pages, n_seq):
    b = pl.program_id(0)
    n_blocks = n_pages // PAGES_PER_BLOCK
    page_flat = ATTN_WIDTH

    def page_copy(seq, p, slot):
        start = pl.multiple_of(pt_ref[seq * n_pages + p] * page_flat, page_flat)
        return pltpu.make_async_copy(ck_hbm.at[pl.ds(start, page_flat), :], buf.at[slot], sem.at[slot])

    @pl.when(b == 0)
    def _():
        for s in range(KMEAN_RING):
            page_copy(0, s, s).start()

    sums_sc[...] = jnp.zeros_like(sums_sc)
    lane = lax.broadcasted_iota(jnp.int32, (1, LANES), 1)
    sub = SUBLANES
    groups = HEAD_DIM // sub

    def block_body(n, carry):
        part = jnp.zeros((N_HEADS, sub, LANES), F32)
        for pg in range(PAGES_PER_BLOCK):
            p = n * PAGES_PER_BLOCK + pg
            slot = p % KMEAN_RING
            page_copy(b, p, slot).wait()
            prod = buf[slot] * q_ref[0]
            part = part + jnp.sum(prod.reshape(N_HEADS, groups, sub, LANES), axis=1)
            ahead = p + KMEAN_RING

            @pl.when(ahead < n_pages)
            def _():
                page_copy(b, ahead, slot).start()

            @pl.when((ahead >= n_pages) & (b + 1 < n_seq))
            def _():
                page_copy(b + 1, ahead - n_pages, slot).start()
        block_sum = jnp.sum(part.reshape(N_HEADS * sub, LANES), axis=1, keepdims=True)
        sums_sc[...] = jnp.where(lane == n, block_sum, sums_sc[...])
        return carry

    lax.fori_loop(0, n_blocks, block_body, 0)

    fold = (lax.broadcasted_iota(jnp.int32, (N_HEADS, N_HEADS * sub), 1) // sub
            == lax.broadcasted_iota(jnp.int32, (N_HEADS, N_HEADS * sub), 0)).astype(F32)
    gate = jnp.dot(fold, sums_sc[...], precision=HIGHEST, preferred_element_type=F32) * (1.0 / MOBA_BLOCK)
    blk = lax.broadcasted_iota(jnp.int32, gate.shape, 1)
    gate = jnp.where(blk < n_blocks, gate, NEG_INF)
    rank = jnp.zeros(gate.shape, jnp.int32)
    for s in range(1, LANES):
        other = pltpu.roll(gate, s, axis=1)
        other_blk = jnp.where(blk >= s, blk - s, blk - s + LANES)
        ahead = (other > gate) | ((other == gate) & (other_blk < blk))
        rank = rank + ahead.astype(jnp.int32)
    out = jnp.zeros(gate.shape, jnp.int32)
    for r in range(MOBA_TOPK):
        picked = jnp.sum(jnp.where(rank == r, blk, 0).astype(F32), axis=1, keepdims=True)
        out = jnp.where(blk == r, picked.astype(jnp.int32), out)
    sel_ref[0] = out


def _kmean_select(page_table_flat, q_lanes, cache_kt, n_seq, n_pages):
    kern = functools.partial(_kmean_select_kernel, n_pages=n_pages, n_seq=n_seq)
    return pl.pallas_call(
        kern,
        grid_spec=pltpu.PrefetchScalarGridSpec(
            num_scalar_prefetch=1,
            grid=(n_seq,),
            in_specs=[
                pl.BlockSpec((1, ATTN_WIDTH, LANES), lambda b, pt: (b, 0, 0)),
                pl.BlockSpec(memory_space=pl.ANY),
            ],
            out_specs=pl.BlockSpec((1, N_HEADS, LANES), lambda b, pt: (b, 0, 0)),
            scratch_shapes=[
                pltpu.VMEM((KMEAN_RING, ATTN_WIDTH, LANES), F32),
                pltpu.SemaphoreType.DMA((KMEAN_RING,)),
                pltpu.VMEM((N_HEADS * SUBLANES, LANES), F32),
            ],
        ),
        out_shape=jax.ShapeDtypeStruct((n_seq, N_HEADS, LANES), jnp.int32),
        compiler_params=_cparams("arbitrary"),
        name="kmean_select_sample",
    )(page_table_flat, q_lanes, cache_kt)


def _moba_sample_kernel(pt_ref, sel_ref, q_ref, kn_ref, vn_ref, ck_hbm, cv_hbm, o_ref,
                        kbuf, vbuf, ksem, vsem, *, n_pages, n_seq):
    b = pl.program_id(0)
    half = b % 2

    def seq_copies(seq, dst_half):
        copies = []
        for h in range(N_HEADS):
            for r in range(MOBA_TOPK):
                blk = sel_ref[(seq * N_HEADS + h) * MOBA_TOPK + r]
                for pg in range(PAGES_PER_BLOCK):
                    page = pt_ref[seq * n_pages + blk * PAGES_PER_BLOCK + pg]
                    src = pl.ds(pl.multiple_of(page * ATTN_WIDTH + h * HEAD_DIM, HEAD_DIM), HEAD_DIM)
                    dst = pl.ds((r * PAGES_PER_BLOCK + pg) * LANES, LANES)
                    copies.append(pltpu.make_async_copy(ck_hbm.at[src, :], kbuf.at[dst_half, h, :, dst],
                                                        ksem.at[dst_half]))
                    copies.append(pltpu.make_async_copy(cv_hbm.at[src, :], vbuf.at[dst_half, h, :, dst],
                                                        vsem.at[dst_half]))
        return copies

    @pl.when(b == 0)
    def _():
        for cp in seq_copies(0, 0):
            cp.start()

    @pl.when(b + 1 < n_seq)
    def _():
        for cp in seq_copies(b + 1, 1 - half):
            cp.start()

    for cp in seq_copies(b, half):
        cp.wait()

    q8 = q_ref[0]
    kn = kn_ref[0]
    vn = vn_ref[0]
    nt = (((1,), (1,)), ((), ()))
    l_past = jnp.concatenate(
        [jnp.dot(q8, kbuf[half, h], precision=HIGHEST, preferred_element_type=F32)[h:h + 1, :]
         for h in range(N_HEADS)], axis=0) * SM_SCALE
    l_self = jnp.sum(q8 * kn, axis=1, keepdims=True) * SM_SCALE
    m = jnp.maximum(jnp.max(l_past, axis=1, keepdims=True), l_self)
    p = jnp.exp(l_past - m)
    p_self = jnp.exp(l_self - m)
    den = jnp.sum(p, axis=1, keepdims=True) + p_self
    pv = jnp.concatenate(
        [lax.dot_general(p, vbuf[half, h], nt, precision=HIGHEST, preferred_element_type=F32)[h:h + 1, :]
         for h in range(N_HEADS)], axis=0)
    o_ref[0] = (p_self * vn + pv) / den


def _moba_sample(page_table_flat, sel_flat, q3, k3, v3, cache_kt, cache_vt, n_seq, n_pages):
    kern = functools.partial(_moba_sample_kernel, n_pages=n_pages, n_seq=n_seq)
    n_keys = MOBA_TOPK * MOBA_BLOCK
    tok = pl.BlockSpec((1, N_HEADS, HEAD_DIM), lambda b, pt, sel: (b, 0, 0))
    return pl.pallas_call(
        kern,
        grid_spec=pltpu.PrefetchScalarGridSpec(
            num_scalar_prefetch=2,
            grid=(n_seq,),
            in_specs=[tok, tok, tok, pl.BlockSpec(memory_space=pl.ANY), pl.BlockSpec(memory_space=pl.ANY)],
            out_specs=tok,
            scratch_shapes=[
                pltpu.VMEM((2, N_HEADS, HEAD_DIM, n_keys), F32),
                pltpu.VMEM((2, N_HEADS, HEAD_DIM, n_keys), F32),
                pltpu.SemaphoreType.DMA((2,)),
                pltpu.SemaphoreType.DMA((2,)),
            ],
        ),
        out_shape=jax.ShapeDtypeStruct((n_seq, N_HEADS, HEAD_DIM), F32),
        compiler_params=_cparams("arbitrary"),
        name="moba_sample",
    )(page_table_flat, sel_flat, q3, k3, v3, cache_kt, cache_vt)


def _token_tile(t, want):
    return want if t % want == 0 else t


def kernel(x_prompt, x_sample, cache_k, cache_v, state_conv, page_table, norm1_g, w_in, b_in, conv_w, conv_b,
           conv_ln_g, conv_ln_b, w_out, norm2_g, w_group, b_group, w_router, b_router, w_gate, w_up, w_down,
           norm_f_g):
    bsz, slen, _ = x_prompt.shape
    dbsz, dlen, _ = x_sample.shape
    depth = w_in.shape[0]
    assert depth == 1 and dlen == 1, "one layer and one new sample token per sequence"
    n_pages, page_rows = page_table.shape[1], cache_k.shape[2]
    assert MOBA_BLOCK == PAGES_PER_BLOCK * page_rows and slen % max(MOBA_BLOCK * PAST_SPAN, TOKEN_TILE) == 0
    assert page_rows == LANES and MOBA_TOPK <= n_pages // PAGES_PER_BLOCK <= LANES and n_pages % KMEAN_RING == 0
    past_len = n_pages * page_rows
    l = 0

    w_in_b = w_in[l].astype(BF16)
    w_out_b = w_out[l].astype(BF16)
    wg_f, wu_f, wd_f = w_gate[l], w_up[l], w_down[l]
    pad_cols = ROUTER_COLS - N_EXPERTS - N_EXPERT_GROUPS
    w_rt = jnp.pad(jnp.concatenate([w_router[l], w_group[l]], axis=1), ((0, 0), (0, pad_cols)))
    b_rt = jnp.pad(jnp.concatenate([b_router[l], b_group[l]])[None, :], ((0, 0), (0, pad_cols)))
    g1, g2, gf = norm1_g[l][None, :], norm2_g[l][None, :], norm_f_g[None, :]
    b_in2, cb = b_in[l][None, :], conv_b[l][None, :]
    lg, lb = conv_ln_g[l][None, :], conv_ln_b[l][None, :]

    t_p = bsz * slen
    xp = x_prompt.reshape(t_p, D_MODEL)
    tm_in = _token_tile(slen, 512)
    q_p, kt_p, vt_p, u_p, kb_p, vtb_p, km_p = _in_projection(xp, g1, w_in_b, b_in2, tm_in, True, 0, 1, slen)
    km_p = km_p.reshape(bsz, slen // MOBA_BLOCK, ATTN_WIDTH)
    attn_p = _moba_prompt(q_p, kb_p, vtb_p, km_p, bsz, slen)
    y_p = _post_mixer_grouped(xp, attn_p, u_p, conv_w[l], cb, lg, lb, w_out_b, g2, w_rt, b_rt, wg_f, wu_f, wd_f, gf,
                              slen)

    xs = x_sample.reshape(dbsz, D_MODEL)
    q_s, k_s, v_s, u_s = _in_projection(xs, g1, w_in_b, b_in2, dbsz, False, past_len, 0, dbsz)
    heads = lambda t: t.reshape(dbsz, N_HEADS, HEAD_DIM)
    ck2 = cache_k[l].transpose(0, 2, 3, 1).reshape(-1, page_rows)
    cv2 = cache_v[l].transpose(0, 2, 3, 1).reshape(-1, page_rows)
    pt_flat = page_table.reshape(-1)
    q_lanes = jnp.broadcast_to(q_s[:, :, None], (dbsz, ATTN_WIDTH, LANES))
    sel = _kmean_select(pt_flat, q_lanes, ck2, dbsz, n_pages)
    sel_flat = sel[:, :, :MOBA_TOPK].reshape(-1)
    attn_s = _moba_sample(pt_flat, sel_flat, heads(q_s), heads(k_s), heads(v_s), ck2, cv2, dbsz, n_pages)
    conv_s = _conv_sample(state_conv[l], u_s, conv_w[l], cb, lg, lb)
    y_s = _post_mixer(xs, attn_s.reshape(dbsz, ATTN_WIDTH).astype(BF16), conv_s, w_out_b, g2, w_rt, b_rt,
                      wg_f, wu_f, wd_f, gf, dbsz, "post_mixer_sample")

    hist = CONV_LEN - 1
    kv_p = lambda t: t.reshape(bsz, N_HEADS, HEAD_DIM, slen).transpose(0, 3, 1, 2)[None]
    kv_s = lambda t: t.reshape(1, dbsz, 1, N_HEADS, HEAD_DIM)
    new_conv_p = u_p.reshape(bsz, slen, CONV_CH)[:, slen - hist:][None]
    new_conv_s = jnp.concatenate([state_conv[l][:, 1:], u_s[:, None, :]], axis=1)[None]
    return (y_p.reshape(bsz, slen, D_MODEL), y_s.reshape(dbsz, 1, D_MODEL), kv_p(kt_p), kv_p(vt_p), new_conv_p,
            kv_s(k_s), kv_s(v_s), new_conv_s)
```

```python
import functools
import math

import jax
import jax.numpy as jnp
import numpy as np
from jax import lax
from jax.experimental import pallas as pl
from jax.experimental.pallas import tpu as pltpu

F32 = jnp.float32
BF16 = jnp.bfloat16
HIGHEST = lax.Precision.HIGHEST

D_MODEL = 1024
ATTN_WIDTH = 512
CONV_CH = 512
HEAD_DIM = 64
N_HEADS = 8
IN_WIDTH = 3 * ATTN_WIDTH + 2 * CONV_CH
CONV_LEN = 31
MOBA_BLOCK = 256
MOBA_TOPK = 3
ROPE_THETA = 10000.0
N_EXPERT_GROUPS = 4
EXPERTS_PER_GROUP = 4
N_EXPERTS = 16
D_EXPERT = 512
RMS_EPS = 1e-6
LN_EPS = 1e-5
NEG_INF = -1e30
SM_SCALE = HEAD_DIM ** -0.5
_ROPE_LOG_STEP = -math.log(ROPE_THETA) / (HEAD_DIM // 2)
ROPE_LOG_STEP_HI = float(np.float32(_ROPE_LOG_STEP))
ROPE_LOG_STEP_LO = _ROPE_LOG_STEP - ROPE_LOG_STEP_HI

LANES = 128
SUBLANES = 8
HEADS_PER_LANE_TILE = LANES // HEAD_DIM
ROUTER_COLS = LANES
GROUP_COL0 = N_EXPERTS
VMEM_LIMIT = 56 * 1024 * 1024
CONV_HALO = 32
CONV_CHUNK = 64
LOG2_E = math.log2(math.e)
PAST_SPAN = 2
DENOM_ROWS = 16
ATTN_TILES = 4
KMEAN_RING = 16
PAGES_PER_BLOCK = 2
MOE_TILE = 512
TOKEN_TILE = 512
GATHER_UNROLL = 8
ROW_WIDTH = D_MODEL + ROUTER_COLS
GROUP_ID_COL = ROUTER_COLS - 1


def _cparams(*sem):
    return pltpu.CompilerParams(dimension_semantics=sem, vmem_limit_bytes=VMEM_LIMIT)


def _inproj_kernel(x_ref, g_ref, w_ref, b_ref, q_ref, k_ref, v_ref, u_ref, *prompt_refs,
                   pos0, pos_stride, n_pos_tiles):
    x = x_ref[...]
    tm = x.shape[0]
    h = x * lax.rsqrt(jnp.mean(x * x, axis=-1, keepdims=True) + RMS_EPS) * g_ref[...]
    z = jnp.dot(h.astype(BF16), w_ref[...], preferred_element_type=F32) + b_ref[...]
    a = ATTN_WIDTH
    reps = a // LANES
    half = HEAD_DIM // 2
    lane_t = lax.broadcasted_iota(jnp.int32, (1, LANES), 1)
    freq_ix = (lane_t % half).astype(F32)
    inv_freq = jnp.exp(freq_ix * ROPE_LOG_STEP_HI) * jnp.exp(freq_ix * ROPE_LOG_STEP_LO)
    tile_pos = pos0 + (pl.program_id(0) % n_pos_tiles) * (tm * pos_stride)
    pos = tile_pos + lax.broadcasted_iota(jnp.int32, (tm, 1), 0) * pos_stride
    ang = pos.astype(F32) * inv_freq
    sin_sign = jnp.where((lane_t % HEAD_DIM) < half, -1.0, 1.0)
    cos = jnp.concatenate([jnp.cos(ang)] * reps, axis=-1)
    sin = jnp.concatenate([jnp.sin(ang) * sin_sign] * reps, axis=-1)
    lane = lax.broadcasted_iota(jnp.int32, (1, a), 1)
    first_half = (lane % HEAD_DIM) < half

    def rope(t):
        from_below = pltpu.roll(t, HEAD_DIM // 2, axis=1)
        from_above = pltpu.roll(t, a - HEAD_DIM // 2, axis=1)
        return t * cos + jnp.where(first_half, from_above, from_below) * sin

    q = rope(z[:, 0:a])
    k = rope(z[:, a:2 * a])
    v = z[:, 2 * a:3 * a]
    u = z[:, 3 * a:3 * a + CONV_CH] * jax.nn.sigmoid(z[:, 3 * a + CONV_CH:])
    q_ref[...] = q
    u_ref[...] = u
    if prompt_refs:
        kb_ref, vtb_ref, km_ref = prompt_refs
        v_t = v.T
        k_ref[...] = k.T
        v_ref[...] = v_t
        kb_ref[...] = k.astype(BF16)
        vtb_ref[...] = v_t.astype(BF16)
        nblk = k.shape[0] // MOBA_BLOCK
        km_ref[0] = jnp.mean(k.reshape(nblk, MOBA_BLOCK, a), axis=1)
    else:
        k_ref[...] = k
        v_ref[...] = v


def _in_projection(x2d, norm_g, w_bf16, b_in, tm, prompt, pos0, pos_stride, seq_len):
    t = x2d.shape[0]
    n_t = t // tm
    kern = functools.partial(_inproj_kernel, pos0=pos0, pos_stride=pos_stride, n_pos_tiles=seq_len // tm)
    row = lambda i: (i, 0)
    fixed = lambda i: (0, 0)
    n_pos = seq_len // tm
    wide = jax.ShapeDtypeStruct((t, ATTN_WIDTH), F32)
    wide_spec = pl.BlockSpec((tm, ATTN_WIDTH), row)
    if prompt:
        nblk = tm // MOBA_BLOCK
        n_seq = t // seq_len
        feat_spec = pl.BlockSpec((ATTN_WIDTH, tm), lambda i: (i // n_pos, i % n_pos))
        feat_f32 = jax.ShapeDtypeStruct((n_seq * ATTN_WIDTH, seq_len), F32)
        feat_bf16 = jax.ShapeDtypeStruct((n_seq * ATTN_WIDTH, seq_len), BF16)
        out_shape = [wide, feat_f32, feat_f32, wide, jax.ShapeDtypeStruct((t, ATTN_WIDTH), BF16), feat_bf16,
                     jax.ShapeDtypeStruct((n_t, nblk, ATTN_WIDTH), F32)]
        out_specs = [wide_spec, feat_spec, feat_spec, wide_spec, wide_spec, feat_spec,
                     pl.BlockSpec((1, nblk, ATTN_WIDTH), lambda i: (i, 0, 0))]
    else:
        out_shape = [wide] * 4
        out_specs = [wide_spec] * 4
    return pl.pallas_call(
        kern,
        grid=(n_t,),
        in_specs=[
            pl.BlockSpec((tm, D_MODEL), row),
            pl.BlockSpec((1, D_MODEL), fixed),
            pl.BlockSpec((D_MODEL, IN_WIDTH), fixed),
            pl.BlockSpec((1, IN_WIDTH), fixed),
        ],
        out_specs=out_specs,
        out_shape=out_shape,
        compiler_params=_cparams("parallel"),
        name="in_projection_prompt" if prompt else "in_projection_sample",
    )(x2d, norm_g, w_bf16, b_in)


def _moba_prompt_kernel(q_ref, kb_ref, vt_ref, km_ref, o_ref, m_sc, acc_sc, bias_sc, s_even, s_odd):
    qi = pl.program_id(2)
    blk_rows = MOBA_BLOCK
    span_rows = PAST_SPAN * blk_rows
    tiles = range(ATTN_TILES)
    cols = lambda t: slice(t * LANES, (t + 1) * LANES)
    nt = (((1,), (1,)), ((), ()))
    lane = lax.broadcasted_iota(jnp.int32, (1, LANES), 1)
    head0 = lane < HEAD_DIM
    n_blocks = km_ref.shape[1]

    qb = []
    for t in tiles:
        q = q_ref[:, cols(t)]
        q2 = jnp.concatenate([jnp.where(head0, q, 0.0), jnp.where(head0, 0.0, q)], axis=0)
        km = km_ref[0, :, cols(t)]
        km_hi, q_hi = km.astype(BF16), q2.astype(BF16)
        km_lo, q_lo = (km - km_hi.astype(F32)).astype(BF16), (q2 - q_hi.astype(F32)).astype(BF16)
        gate = (lax.dot_general(km_hi, q_hi, nt, preferred_element_type=F32)
                + lax.dot_general(km_lo, q_hi, nt, preferred_element_type=F32)
                + lax.dot_general(km_hi, q_lo, nt, preferred_element_type=F32))
        blk = lax.broadcasted_iota(jnp.int32, gate.shape, 0)
        fully_past = blk < qi
        gate = jnp.where(fully_past, gate, NEG_INF)
        rank = jnp.zeros(gate.shape, jnp.int32)
        for m in range(n_blocks):
            gm = gate[m:m + 1, :]
            ahead = (gm > gate) | ((gm == gate) & (m < blk))
            rank = rank + ahead.astype(jnp.int32)
        bias_sc[t] = jnp.where(fully_past & (rank < MOBA_TOPK), 0.0, NEG_INF)
        qb.append((q2 * (SM_SCALE * LOG2_E)).astype(BF16))

    def scores(t, j0):
        r0 = pl.multiple_of(j0 * blk_rows, blk_rows)
        s = lax.dot_general(kb_ref[pl.ds(r0, span_rows), cols(t)], qb[t], nt, preferred_element_type=F32)
        return jnp.concatenate([s[i * blk_rows:(i + 1) * blk_rows] + bias_sc[t, pl.ds(j0 + i, 1), :]
                                for i in range(PAST_SPAN)], axis=0)

    def with_ones(vt):
        return jnp.concatenate([vt, jnp.ones((DENOM_ROWS, vt.shape[1]), BF16)], axis=0)

    def fold(t, s, j0):
        r0 = pl.multiple_of(j0 * blk_rows, blk_rows)
        m_old = m_sc[t]
        m_new = jnp.maximum(m_old, jnp.max(s, axis=0, keepdims=True))
        alpha = jnp.exp2(m_old - m_new)
        p = jnp.exp2(s - m_new)
        acc_sc[t] = alpha * acc_sc[t] + jnp.dot(with_ones(vt_ref[cols(t), pl.ds(r0, span_rows)]), p.astype(BF16),
                                                preferred_element_type=F32)
        m_sc[t] = m_new

    n_spans = (qi + PAST_SPAN - 1) // PAST_SPAN
    last_span = jnp.maximum(n_spans - 1, 0)
    span0 = lambda i: jnp.minimum(i, last_span) * PAST_SPAN
    for t in tiles:
        s_even[t] = scores(t, 0)

    own0 = pl.multiple_of(qi * blk_rows, blk_rows)
    for t in tiles:
        s = lax.dot_general(kb_ref[pl.ds(own0, blk_rows), cols(t)], qb[t], nt, preferred_element_type=F32)
        k_ix = lax.broadcasted_iota(jnp.int32, s.shape, 0)
        q_ix = lax.broadcasted_iota(jnp.int32, s.shape, 1) % blk_rows
        s = jnp.where(k_ix <= q_ix, s, NEG_INF)
        m0 = jnp.max(s, axis=0, keepdims=True)
        p = jnp.exp2(s - m0)
        m_sc[t] = m0
        acc_sc[t] = jnp.dot(with_ones(vt_ref[cols(t), pl.ds(own0, blk_rows)]), p.astype(BF16),
                            preferred_element_type=F32)

    def pair_body(u, carry):
        i = 2 * u
        for t in tiles:
            s_odd[t] = scores(t, span0(i + 1))
        for t in tiles:
            fold(t, s_even[t], i * PAST_SPAN)
        for t in tiles:
            s_even[t] = scores(t, span0(i + 2))
        for t in tiles:
            fold(t, s_odd[t], (i + 1) * PAST_SPAN)
        return carry

    lax.fori_loop(0, n_spans // 2, pair_body, 0)

    @pl.when(n_spans % 2 == 1)
    def _():
        for t in tiles:
            fold(t, s_even[t], (n_spans - 1) * PAST_SPAN)

    for t in tiles:
        out = acc_sc[t, 0:LANES] / acc_sc[t, LANES:LANES + 1]
        out_t = jnp.concatenate([out[:HEAD_DIM, :blk_rows], out[HEAD_DIM:, blk_rows:]], axis=0)
        o_ref[:, cols(t)] = out_t.T.astype(o_ref.dtype)


def _moba_prompt(q, kb, vtb, kmean, bsz, slen):
    n_q = slen // MOBA_BLOCK
    width = ATTN_TILES * LANES
    n_w = ATTN_WIDTH // width
    cols2 = HEADS_PER_LANE_TILE * MOBA_BLOCK
    return pl.pallas_call(
        _moba_prompt_kernel,
        grid=(bsz, n_w, n_q),
        in_specs=[
            pl.BlockSpec((MOBA_BLOCK, width), lambda b, w, qi: (b * n_q + qi, w)),
            pl.BlockSpec((slen, width), lambda b, w, qi: (b, w)),
            pl.BlockSpec((width, slen), lambda b, w, qi: (b * n_w + w, 0)),
            pl.BlockSpec((1, n_q, width), lambda b, w, qi: (b, 0, w)),
        ],
        out_specs=pl.BlockSpec((MOBA_BLOCK, width), lambda b, w, qi: (b * n_q + qi, w)),
        out_shape=jax.ShapeDtypeStruct((bsz * slen, ATTN_WIDTH), BF16),
        scratch_shapes=[
            pltpu.VMEM((ATTN_TILES, 1, cols2), F32),
            pltpu.VMEM((ATTN_TILES, LANES + DENOM_ROWS, cols2), F32),
            pltpu.VMEM((ATTN_TILES, n_q, cols2), F32),
            pltpu.VMEM((ATTN_TILES, PAST_SPAN * MOBA_BLOCK, cols2), F32),
            pltpu.VMEM((ATTN_TILES, PAST_SPAN * MOBA_BLOCK, cols2), F32),
        ],
        compiler_params=_cparams("parallel", "parallel", "arbitrary"),
        name="moba_prompt",
    )(q, kb, vtb, kmean)


def _ln_swish(c, g, b):
    mu = jnp.mean(c, axis=-1, keepdims=True)
    d = c - mu
    var = jnp.mean(d * d, axis=-1, keepdims=True)
    y = d * lax.rsqrt(var + LN_EPS) * g + b
    return y * jax.nn.sigmoid(y)


def _conv_tile(first_tile, prev_ref, cur_ref, w_ref, cb_ref, g_ref, b_ref, ext_sc, store):
    tc = cur_ref.shape[0]
    ext_sc[0:CONV_HALO, :] = jnp.where(first_tile, 0.0, prev_ref[...])
    ext_sc[CONV_HALO:CONV_HALO + tc, :] = cur_ref[...]
    ext_sc[CONV_HALO + tc:CONV_HALO + tc + SUBLANES, :] = jnp.zeros((SUBLANES, CONV_CH), F32)
    lead = CONV_HALO - (CONV_LEN - 1)
    for r0 in range(0, tc, CONV_CHUNK):
        acc = jnp.zeros((CONV_CHUNK, CONV_CH), F32)
        for res in range(SUBLANES):
            taps = [j for j in range(CONV_LEN) if (j + lead) % SUBLANES == res]
            part = None
            for j in taps:
                base = r0 + j + lead - res
                term = w_ref[j:j + 1, :] * ext_sc[base:base + CONV_CHUNK + SUBLANES, :]
                part = term if part is None else part + term
            if part is not None:
                acc = acc + part[res:res + CONV_CHUNK]
        store(r0, _ln_swish(acc + cb_ref[...], g_ref[...], b_ref[...]))


def _conv_sample_kernel(st_ref, u_ref, w_ref, cb_ref, g_ref, b_ref, o_ref):
    hist = CONV_LEN - 1
    for b in range(st_ref.shape[0]):
        u_row = u_ref[b:b + 1, :]
        c = jnp.sum(st_ref[b] * w_ref[0:hist, :], axis=0, keepdims=True) + w_ref[hist:hist + 1, :] * u_row
        y = _ln_swish(c + cb_ref[...], g_ref[...], b_ref[...])
        o_ref[b:b + 1, :] = y.astype(o_ref.dtype)


def _conv_sample(state, u2d, conv_w, conv_b, ln_g, ln_b):
    n = u2d.shape[0]
    return pl.pallas_call(
        _conv_sample_kernel,
        out_shape=jax.ShapeDtypeStruct((n, CONV_CH), BF16),
        compiler_params=pltpu.CompilerParams(vmem_limit_bytes=VMEM_LIMIT),
        name="conv_sample",
    )(state, u2d, conv_w, conv_b, ln_g, ln_b)


def _route(logits, with_group=False):
    c = lax.broadcasted_iota(jnp.int32, logits.shape, 1)
    big = jnp.int32(ROUTER_COLS)
    is_group = (c >= GROUP_COL0) & (c < GROUP_COL0 + N_EXPERT_GROUPS)
    gl = jnp.where(is_group, logits, NEG_INF)
    gmax = jnp.max(gl, axis=1, keepdims=True)
    g_sel = jnp.min(jnp.where(gl == gmax, c - GROUP_COL0, big), axis=1, keepdims=True)
    g_w = 1.0 / jnp.sum(jnp.exp(gl - gmax), axis=1, keepdims=True)
    in_group = (c < N_EXPERTS) & ((c // EXPERTS_PER_GROUP) == g_sel)
    el = jnp.where(in_group, logits, NEG_INF)
    m1 = jnp.max(el, axis=1, keepdims=True)
    i1 = jnp.min(jnp.where(el == m1, c, big), axis=1, keepdims=True)
    el2 = jnp.where(c == i1, NEG_INF, el)
    m2 = jnp.max(el2, axis=1, keepdims=True)
    i2 = jnp.min(jnp.where(el2 == m2, c, big), axis=1, keepdims=True)
    e21 = jnp.exp(m2 - m1)
    inv = 1.0 / (1.0 + e21)
    comb = g_w * jnp.where(c == i1, inv, jnp.where(c == i2, e21 * inv, 0.0))
    return (comb, g_sel) if with_group else comb


def _post_kernel(x_ref, a_ref, c_ref, wo_ref, g2_ref, wr_ref, br_ref, wg_ref, wu_ref, wd_ref, gf_ref,
                 y_ref, x1_sc, h2_sc, comb_sc, acc_sc):
    e = pl.program_id(1)

    @pl.when(e == 0)
    def _():
        x1 = (x_ref[...]
              + jnp.dot(a_ref[...], wo_ref[0:ATTN_WIDTH, :], preferred_element_type=F32)
              + jnp.dot(c_ref[...], wo_ref[ATTN_WIDTH:D_MODEL, :], preferred_element_type=F32))
        x1_sc[...] = x1
        h2 = x1 * lax.rsqrt(jnp.mean(x1 * x1, axis=-1, keepdims=True) + RMS_EPS) * g2_ref[...]
        h2_sc[...] = h2.astype(BF16)
        logits = jnp.dot(h2, wr_ref[...], precision=HIGHEST, preferred_element_type=F32) + br_ref[...]
        comb_sc[...] = _route(logits)
        acc_sc[...] = jnp.zeros_like(acc_sc)

    h2b = h2_sc[...]
    col = lax.broadcasted_iota(jnp.int32, comb_sc.shape, 1)
    ce = jnp.sum(jnp.where(col == e, comb_sc[...], 0.0), axis=1, keepdims=True)
    gt = jnp.dot(h2b, wg_ref[0].astype(BF16), preferred_element_type=F32)
    up = jnp.dot(h2b, wu_ref[0].astype(BF16), preferred_element_type=F32)
    hid = gt * jax.nn.sigmoid(gt) * up * ce
    acc_sc[...] += jnp.dot(hid.astype(BF16), wd_ref[0].astype(BF16), preferred_element_type=F32)

    @pl.when(e == N_EXPERTS - 1)
    def _():
        x2 = x1_sc[...] + acc_sc[...]
        y_ref[...] = x2 * lax.rsqrt(jnp.mean(x2 * x2, axis=-1, keepdims=True) + RMS_EPS) * gf_ref[...]


def _post_mixer(x2d, attn, conv, w_out, g2, w_rt, b_rt, wg, wu, wd, gf, tm, name):
    t = x2d.shape[0]
    row = lambda i, e: (i, 0)
    fixed = lambda i, e: (0, 0)
    return pl.pallas_call(
        _post_kernel,
        grid=(t // tm, N_EXPERTS),
        in_specs=[
            pl.BlockSpec((tm, D_MODEL), row),
            pl.BlockSpec((tm, ATTN_WIDTH), row),
            pl.BlockSpec((tm, CONV_CH), row),
            pl.BlockSpec((D_MODEL, D_MODEL), fixed),
            pl.BlockSpec((1, D_MODEL), fixed),
            pl.BlockSpec((D_MODEL, ROUTER_COLS), fixed),
            pl.BlockSpec((1, ROUTER_COLS), fixed),
            pl.BlockSpec((1, D_MODEL, D_EXPERT), lambda i, e: (e, 0, 0)),
            pl.BlockSpec((1, D_MODEL, D_EXPERT), lambda i, e: (e, 0, 0)),
            pl.BlockSpec((1, D_EXPERT, D_MODEL), lambda i, e: (e, 0, 0)),
            pl.BlockSpec((1, D_MODEL), fixed),
        ],
        out_specs=pl.BlockSpec((tm, D_MODEL), row),
        out_shape=jax.ShapeDtypeStruct((t, D_MODEL), F32),
        scratch_shapes=[
            pltpu.VMEM((tm, D_MODEL), F32),
            pltpu.VMEM((tm, D_MODEL), BF16),
            pltpu.VMEM((tm, ROUTER_COLS), F32),
            pltpu.VMEM((tm, D_MODEL), F32),
        ],
        compiler_params=_cparams("parallel", "arbitrary"),
        name=name,
    )(x2d, attn, conv, w_out, g2, w_rt, b_rt, wg, wu, wd, gf)


def _mix_route_kernel(x_ref, a_ref, uprev_ref, u_ref, cw_ref, cb_ref, lg_ref, lb_ref, wo_ref, g2_ref, wr_ref, br_ref,
                      x1_ref, row_ref, ext_sc, conv_sc, *, tiles_per_seq):
    def store(r0, y):
        conv_sc[r0:r0 + CONV_CHUNK, :] = y.astype(BF16)

    first_tile = pl.program_id(0) % tiles_per_seq == 0
    _conv_tile(first_tile, uprev_ref, u_ref, cw_ref, cb_ref, lg_ref, lb_ref, ext_sc, store)
    x1 = (x_ref[...]
          + jnp.dot(a_ref[...], wo_ref[0:ATTN_WIDTH, :], preferred_element_type=F32)
          + jnp.dot(conv_sc[...], wo_ref[ATTN_WIDTH:D_MODEL, :], preferred_element_type=F32))
    x1_ref[...] = x1
    h2 = x1 * lax.rsqrt(jnp.mean(x1 * x1, axis=-1, keepdims=True) + RMS_EPS) * g2_ref[...]
    h_hi = h2.astype(BF16)
    h_lo = (h2 - h_hi.astype(F32)).astype(BF16)
    w = wr_ref[...]
    w_hi = w.astype(BF16)
    w_lo = (w - w_hi.astype(F32)).astype(BF16)
    logits = (jnp.dot(h_hi, w_hi, preferred_element_type=F32) + jnp.dot(h_lo, w_hi, preferred_element_type=F32)
              + jnp.dot(h_hi, w_lo, preferred_element_type=F32) + br_ref[...])
    comb, g_sel = _route(logits, with_group=True)
    col = lax.broadcasted_iota(jnp.int32, comb.shape, 1)
    row_ref[:, 0:D_MODEL] = h2
    row_ref[:, D_MODEL:ROW_WIDTH] = jnp.where(col == GROUP_ID_COL, g_sel.astype(F32), comb)


def _mix_route(x2d, attn, u2d, conv_w, conv_b, ln_g, ln_b, w_out, g2, w_rt, b_rt, tm, slen):
    t = x2d.shape[0]
    tiles_per_seq = slen // tm
    halo_per_tile = tm // CONV_HALO
    row = lambda i: (i, 0)
    fixed = lambda i: (0, 0)
    kern = functools.partial(_mix_route_kernel, tiles_per_seq=tiles_per_seq)
    return pl.pallas_call(
        kern,
        grid=(t // tm,),
        in_specs=[
            pl.BlockSpec((tm, D_MODEL), row),
            pl.BlockSpec((tm, ATTN_WIDTH), row),
            pl.BlockSpec((CONV_HALO, CONV_CH), lambda i: (jnp.maximum(i * halo_per_tile - 1, 0), 0)),
            pl.BlockSpec((tm, CONV_CH), row),
            pl.BlockSpec((CONV_LEN, CONV_CH), fixed),
            pl.BlockSpec((1, CONV_CH), fixed),
            pl.BlockSpec((1, CONV_CH), fixed),
            pl.BlockSpec((1, CONV_CH), fixed),
            pl.BlockSpec((D_MODEL, D_MODEL), fixed),
            pl.BlockSpec((1, D_MODEL), fixed),
            pl.BlockSpec((D_MODEL, ROUTER_COLS), fixed),
            pl.BlockSpec((1, ROUTER_COLS), fixed),
        ],
        out_specs=[pl.BlockSpec((tm, D_MODEL), row), pl.BlockSpec((tm, ROW_WIDTH), row)],
        out_shape=[jax.ShapeDtypeStruct((t, D_MODEL), F32), jax.ShapeDtypeStruct((t, ROW_WIDTH), F32)],
        scratch_shapes=[pltpu.VMEM((CONV_HALO + tm + SUBLANES, CONV_CH), F32), pltpu.VMEM((tm, CONV_CH), BF16)],
        compiler_params=_cparams("parallel"),
        name="conv_mix_route_prompt",
    )(x2d, attn, u2d, u2d, conv_w, conv_b, ln_g, ln_b, w_out, g2, w_rt, b_rt)


def _gather_rows(idx_ref, base, src_hbm, dst, sem, n_rows):
    def body(c, carry):
        for k in range(GATHER_UNROLL):
            r = c * GATHER_UNROLL + k
            pltpu.make_async_copy(src_hbm.at[pl.ds(idx_ref[base + r], 1), :], dst.at[pl.ds(r, 1), :], sem).start()
        return carry
    lax.fori_loop(0, n_rows // GATHER_UNROLL, body, 0)


def _wait_rows(src_hbm, dst, sem, n_rows):
    def body(c, carry):
        for k in range(GATHER_UNROLL):
            r = c * GATHER_UNROLL + k
            pltpu.make_async_copy(src_hbm.at[pl.ds(0, 1), :], dst.at[pl.ds(r, 1), :], sem).wait()
        return carry
    lax.fori_loop(0, n_rows // GATHER_UNROLL, body, 0)


def _moe_grouped_kernel(tg_ref, src_ref, rows_hbm, wg_ref, wu_ref, wd_ref, y_ref, xbuf, sem, hb_sc, acc_sc,
                        *, n_tiles):
    i = pl.program_id(0)
    j = pl.program_id(1)
    slot = i % 2
    group = tg_ref[i]

    @pl.when(j == 0)
    def _():
        @pl.when(i == 0)
        def _():
            _gather_rows(src_ref, 0, rows_hbm, xbuf.at[0], sem.at[0], MOE_TILE)

        @pl.when(i + 1 < n_tiles)
        def _():
            _gather_rows(src_ref, (i + 1) * MOE_TILE, rows_hbm, xbuf.at[1 - slot], sem.at[1 - slot], MOE_TILE)

        _wait_rows(rows_hbm, xbuf.at[slot], sem.at[slot], MOE_TILE)
        hb_sc[...] = xbuf[slot, :, 0:D_MODEL].astype(BF16)
        acc_sc[...] = jnp.zeros_like(acc_sc)

    @pl.when(group < N_EXPERT_GROUPS)
    def _():
        e = group * EXPERTS_PER_GROUP + j
        comb = xbuf[slot, :, D_MODEL:ROW_WIDTH]
        col = lax.broadcasted_iota(jnp.int32, comb.shape, 1)
        ce = jnp.sum(jnp.where(col == e, comb, 0.0), axis=1, keepdims=True)
        hb = hb_sc[...]
        gt = jnp.dot(hb, wg_ref[0].astype(BF16), preferred_element_type=F32)
        up = jnp.dot(hb, wu_ref[0].astype(BF16), preferred_element_type=F32)
        hid = gt * jax.nn.sigmoid(gt) * up * ce
        acc_sc[...] += jnp.dot(hid.astype(BF16), wd_ref[0].astype(BF16), preferred_element_type=F32)

    @pl.when(j == EXPERTS_PER_GROUP - 1)
    def _():
        y_ref[...] = acc_sc[...]


def _moe_grouped(tile_group, src_of_sorted, rows, wg, wu, wd, n_tiles):
    kern = functools.partial(_moe_grouped_kernel, n_tiles=n_tiles)
    last = N_EXPERT_GROUPS - 1
    w_ix = lambda i, j, tg, src: (jnp.minimum(tg[i], last) * EXPERTS_PER_GROUP + j, 0, 0)
    return pl.pallas_call(
        kern,
        grid_spec=pltpu.PrefetchScalarGridSpec(
            num_scalar_prefetch=2,
            grid=(n_tiles, EXPERTS_PER_GROUP),
            in_specs=[
                pl.BlockSpec(memory_space=pl.ANY),
                pl.BlockSpec((1, D_MODEL, D_EXPERT), w_ix),
                pl.BlockSpec((1, D_MODEL, D_EXPERT), w_ix),
                pl.BlockSpec((1, D_EXPERT, D_MODEL), w_ix),
            ],
            out_specs=pl.BlockSpec((MOE_TILE, D_MODEL), lambda i, j, tg, src: (i, 0)),
            scratch_shapes=[
                pltpu.VMEM((2, MOE_TILE, ROW_WIDTH), F32),
                pltpu.SemaphoreType.DMA((2,)),
                pltpu.VMEM((MOE_TILE, D_MODEL), BF16),
                pltpu.VMEM((MOE_TILE, D_MODEL), F32),
            ],
        ),
        out_shape=jax.ShapeDtypeStruct((n_tiles * MOE_TILE, D_MODEL), F32),
        compiler_params=_cparams("arbitrary", "arbitrary"),
        name="moe_grouped_prompt",
    )(tile_group, src_of_sorted, rows, wg, wu, wd)


def _moe_finish_kernel(pos_ref, x1_ref, gf_ref, ys_hbm, y_ref, buf, sem, *, n_tiles):
    i = pl.program_id(0)
    slot = i % 2
    tm = x1_ref.shape[0]

    @pl.when(i == 0)
    def _():
        _gather_rows(pos_ref, 0, ys_hbm, buf.at[0], sem.at[0], tm)

    @pl.when(i + 1 < n_tiles)
    def _():
        _gather_rows(pos_ref, (i + 1) * tm, ys_hbm, buf.at[1 - slot], sem.at[1 - slot], tm)

    _wait_rows(ys_hbm, buf.at[slot], sem.at[slot], tm)
    x2 = x1_ref[...] + buf[slot]
    y_ref[...] = x2 * lax.rsqrt(jnp.mean(x2 * x2, axis=-1, keepdims=True) + RMS_EPS) * gf_ref[...]


def _moe_finish(pos_of_token, x1, gf, ys, tm):
    t = x1.shape[0]
    n_tiles = t // tm
    kern = functools.partial(_moe_finish_kernel, n_tiles=n_tiles)
    return pl.pallas_call(
        kern,
        grid_spec=pltpu.PrefetchScalarGridSpec(
            num_scalar_prefetch=1,
            grid=(n_tiles,),
            in_specs=[
                pl.BlockSpec((tm, D_MODEL), lambda i, pos: (i, 0)),
                pl.BlockSpec((1, D_MODEL), lambda i, pos: (0, 0)),
                pl.BlockSpec(memory_space=pl.ANY),
            ],
            out_specs=pl.BlockSpec((tm, D_MODEL), lambda i, pos: (i, 0)),
            scratch_shapes=[pltpu.VMEM((2, tm, D_MODEL), F32), pltpu.SemaphoreType.DMA((2,))],
        ),
        out_shape=jax.ShapeDtypeStruct((t, D_MODEL), F32),
        compiler_params=_cparams("arbitrary"),
        name="moe_finish_prompt",
    )(pos_of_token, x1, gf, ys)


def _sort_plan(group_id, n_tiles):
    t = group_id.shape[0]
    onehot = (group_id[:, None] == jnp.arange(N_EXPERT_GROUPS, dtype=jnp.int32)[None, :]).astype(jnp.int32)
    counts = jnp.sum(onehot, axis=0)
    rank = jnp.sum((jnp.cumsum(onehot, axis=0) - onehot) * onehot, axis=1)
    padded = (counts + MOE_TILE - 1) // MOE_TILE * MOE_TILE
    group_end = jnp.cumsum(padded)
    group_start = group_end - padded
    pos = group_start[group_id] + rank
    src = jnp.zeros((n_tiles * MOE_TILE,), jnp.int32).at[pos].set(jnp.arange(t, dtype=jnp.int32))
    tile_start = jnp.arange(n_tiles, dtype=jnp.int32) * MOE_TILE
    tile_group = jnp.sum((tile_start[:, None] >= group_end[None, :]).astype(jnp.int32), axis=1)
    return pos.astype(jnp.int32), src, tile_group.astype(jnp.int32)


def _post_mixer_grouped(x2d, attn, u2d, conv_w, conv_b, ln_g, ln_b, w_out_b, g2, w_rt, b_rt, wg, wu, wd, gf, slen):
    t = x2d.shape[0]
    n_tiles = t // MOE_TILE + N_EXPERT_GROUPS
    x1, rows = _mix_route(x2d, attn, u2d, conv_w, conv_b, ln_g, ln_b, w_out_b, g2, w_rt, b_rt, TOKEN_TILE, slen)
    group_id = rows[:, D_MODEL + GROUP_ID_COL].astype(jnp.int32)
    pos, src, tile_group = _sort_plan(group_id, n_tiles)
    ys = _moe_grouped(tile_group, src, rows, wg, wu, wd, n_tiles)
    return _moe_finish(pos, x1, gf, ys, TOKEN_TILE)


def _kmean_select_kernel(pt_ref, q_ref, ck_hbm, sel_ref, buf, sem, sums_sc, *, n_pages, n_seq):
    b = pl.program_id(0)
    n_blocks = n_pages // PAGES_PER_BLOCK
    page_flat = ATTN_WIDTH

    def page_copy(seq, p, slot):
        start = pl.multiple_of(pt_ref[seq * n_pages + p] * page_flat, page_flat)
        return pltpu.make_async_copy(ck_hbm.at[pl.ds(start, page_flat), :], buf.at[slot], sem.at[slot])

    @pl.when(b == 0)
    def _():
        for s in range(KMEAN_RING):
            page_copy(0, s, s).start()

    sums_sc[...] = jnp.zeros_like(sums_sc)
    lane = lax.broadcasted_iota(jnp.int32, (1, LANES), 1)
    sub = SUBLANES
    groups = HEAD_DIM // sub

    def block_body(n, carry):
        part = jnp.zeros((N_HEADS, sub, LANES), F32)
        for pg in range(PAGES_PER_BLOCK):
            p = n * PAGES_PER_BLOCK + pg
            slot = p % KMEAN_RING
            page_copy(b, p, slot).wait()
            prod = buf[slot] * q_ref[0]
            part = part + jnp.sum(prod.reshape(N_HEADS, groups, sub, LANES), axis=1)
            ahead = p + KMEAN_RING

            @pl.when(ahead < n_pages)
            def _():
                page_copy(b, ahead, slot).start()

            @pl.when((ahead >= n_pages) & (b + 1 < n_seq))
            def _():
                page_copy(b + 1, ahead - n_pages, slot).start()
        block_sum = jnp.sum(part.reshape(N_HEADS * sub, LANES), axis=1, keepdims=True)
        sums_sc[...] = jnp.where(lane == n, block_sum, sums_sc[...])
        return carry

    lax.fori_loop(0, n_blocks, block_body, 0)

    fold = (lax.broadcasted_iota(jnp.int32, (N_HEADS, N_HEADS * sub), 1) // sub
            == lax.broadcasted_iota(jnp.int32, (N_HEADS, N_HEADS * sub), 0)).astype(F32)
    gate = jnp.dot(fold, sums_sc[...], precision=HIGHEST, preferred_element_type=F32) * (1.0 / MOBA_BLOCK)
    blk = lax.broadcasted_iota(jnp.int32, gate.shape, 1)
    gate = jnp.where(blk < n_blocks, gate, NEG_INF)
    rank = jnp.zeros(gate.shape, jnp.int32)
    for s in range(1, LANES):
        other = pltpu.roll(gate, s, axis=1)
        other_blk = jnp.where(blk >= s, blk - s, blk - s + LANES)
        ahead = (other > gate) | ((other == gate) & (other_blk < blk))
        rank = rank + ahead.astype(jnp.int32)
    out = jnp.zeros(gate.shape, jnp.int32)
    for r in range(MOBA_TOPK):
        picked = jnp.sum(jnp.where(rank == r, blk, 0).astype(F32), axis=1, keepdims=True)
        out = jnp.where(blk == r, picked.astype(jnp.int32), out)
    sel_ref[0] = out


def _kmean_select(page_table_flat, q_lanes, cache_kt, n_seq, n_pages):
    kern = functools.partial(_kmean_select_kernel, n_pages=n_pages, n_seq=n_seq)
    return pl.pallas_call(
        kern,
        grid_spec=pltpu.PrefetchScalarGridSpec(
            num_scalar_prefetch=1,
            grid=(n_seq,),
            in_specs=[
                pl.BlockSpec((1, ATTN_WIDTH, LANES), lambda b, pt: (b, 0, 0)),
                pl.BlockSpec(memory_space=pl.ANY),
            ],
            out_specs=pl.BlockSpec((1, N_HEADS, LANES), lambda b, pt: (b, 0, 0)),
            scratch_shapes=[
                pltpu.VMEM((KMEAN_RING, ATTN_WIDTH, LANES), F32),
                pltpu.SemaphoreType.DMA((KMEAN_RING,)),
                pltpu.VMEM((N_HEADS * SUBLANES, LANES), F32),
            ],
        ),
        out_shape=jax.ShapeDtypeStruct((n_seq, N_HEADS, LANES), jnp.int32),
        compiler_params=_cparams("arbitrary"),
        name="kmean_select_sample",
    )(page_table_flat, q_lanes, cache_kt)


def _moba_sample_kernel(pt_ref, sel_ref, q_ref, kn_ref, vn_ref, ck_hbm, cv_hbm, o_ref,
                        kbuf, vbuf, ksem, vsem, *, n_pages, n_seq):
    b = pl.program_id(0)
    half = b % 2

    def seq_copies(seq, dst_half):
        copies = []
        for h in range(N_HEADS):
            for r in range(MOBA_TOPK):
                blk = sel_ref[(seq * N_HEADS + h) * MOBA_TOPK + r]
                for pg in range(PAGES_PER_BLOCK):
                    page = pt_ref[seq * n_pages + blk * PAGES_PER_BLOCK + pg]
                    src = pl.ds(pl.multiple_of(page * ATTN_WIDTH + h * HEAD_DIM, HEAD_DIM), HEAD_DIM)
                    dst = pl.ds((r * PAGES_PER_BLOCK + pg) * LANES, LANES)
                    copies.append(pltpu.make_async_copy(ck_hbm.at[src, :], kbuf.at[dst_half, h, :, dst],
                                                        ksem.at[dst_half]))
                    copies.append(pltpu.make_async_copy(cv_hbm.at[src, :], vbuf.at[dst_half, h, :, dst],
                                                        vsem.at[dst_half]))
        return copies

    @pl.when(b == 0)
    def _():
        for cp in seq_copies(0, 0):
            cp.start()

    @pl.when(b + 1 < n_seq)
    def _():
        for cp in seq_copies(b + 1, 1 - half):
            cp.start()

    for cp in seq_copies(b, half):
        cp.wait()

    q8 = q_ref[0]
    kn = kn_ref[0]
    vn = vn_ref[0]
    nt = (((1,), (1,)), ((), ()))
    l_past = jnp.concatenate(
        [jnp.dot(q8, kbuf[half, h], precision=HIGHEST, preferred_element_type=F32)[h:h + 1, :]
         for h in range(N_HEADS)], axis=0) * SM_SCALE
    l_self = jnp.sum(q8 * kn, axis=1, keepdims=True) * SM_SCALE
    m = jnp.maximum(jnp.max(l_past, axis=1, keepdims=True), l_self)
    p = jnp.exp(l_past - m)
    p_self = jnp.exp(l_self - m)
    den = jnp.sum(p, axis=1, keepdims=True) + p_self
    pv = jnp.concatenate(
        [lax.dot_general(p, vbuf[half, h], nt, precision=HIGHEST, preferred_element_type=F32)[h:h + 1, :]
         for h in range(N_HEADS)], axis=0)
    o_ref[0] = (p_self * vn + pv) / den


def _moba_sample(page_table_flat, sel_flat, q3, k3, v3, cache_kt, cache_vt, n_seq, n_pages):
    kern = functools.partial(_moba_sample_kernel, n_pages=n_pages, n_seq=n_seq)
    n_keys = MOBA_TOPK * MOBA_BLOCK
    tok = pl.BlockSpec((1, N_HEADS, HEAD_DIM), lambda b, pt, sel: (b, 0, 0))
    return pl.pallas_call(
        kern,
        grid_spec=pltpu.PrefetchScalarGridSpec(
            num_scalar_prefetch=2,
            grid=(n_seq,),
            in_specs=[tok, tok, tok, pl.BlockSpec(memory_space=pl.ANY), pl.BlockSpec(memory_space=pl.ANY)],
            out_specs=tok,
            scratch_shapes=[
                pltpu.VMEM((2, N_HEADS, HEAD_DIM, n_keys), F32),
                pltpu.VMEM((2, N_HEADS, HEAD_DIM, n_keys), F32),
                pltpu.SemaphoreType.DMA((2,)),
                pltpu.SemaphoreType.DMA((2,)),
            ],
        ),
        out_shape=jax.ShapeDtypeStruct((n_seq, N_HEADS, HEAD_DIM), F32),
        compiler_params=_cparams("arbitrary"),
        name="moba_sample",
    )(page_table_flat, sel_flat, q3, k3, v3, cache_kt, cache_vt)


def _token_tile(t, want):
    return want if t % want == 0 else t


def kernel(x_prompt, x_sample, cache_k, cache_v, state_conv, page_table, norm1_g, w_in, b_in, conv_w, conv_b,
           conv_ln_g, conv_ln_b, w_out, norm2_g, w_group, b_group, w_router, b_router, w_gate, w_up, w_down,
           norm_f_g):
    bsz, slen, _ = x_prompt.shape
    dbsz, dlen, _ = x_sample.shape
    depth = w_in.shape[0]
    assert depth == 1 and dlen == 1, "one layer and one new sample token per sequence"
    n_pages, page_rows = page_table.shape[1], cache_k.shape[2]
    assert MOBA_BLOCK == PAGES_PER_BLOCK * page_rows and slen % max(MOBA_BLOCK * PAST_SPAN, TOKEN_TILE) == 0
    assert page_rows == LANES and MOBA_TOPK <= n_pages // PAGES_PER_BLOCK <= LANES and n_pages % KMEAN_RING == 0
    past_len = n_pages * page_rows
    l = 0

    w_in_b = w_in[l].astype(BF16)
    w_out_b = w_out[l].astype(BF16)
    wg_f, wu_f, wd_f = w_gate[l], w_up[l], w_down[l]
    pad_cols = ROUTER_COLS - N_EXPERTS - N_EXPERT_GROUPS
    w_rt = jnp.pad(jnp.concatenate([w_router[l], w_group[l]], axis=1), ((0, 0), (0, pad_cols)))
    b_rt = jnp.pad(jnp.concatenate([b_router[l], b_group[l]])[None, :], ((0, 0), (0, pad_cols)))
    g1, g2, gf = norm1_g[l][None, :], norm2_g[l][None, :], norm_f_g[None, :]
    b_in2, cb = b_in[l][None, :], conv_b[l][None, :]
    lg, lb = conv_ln_g[l][None, :], conv_ln_b[l][None, :]

    t_p = bsz * slen
    xp = x_prompt.reshape(t_p, D_MODEL)
    tm_in = _token_tile(slen, 512)
    q_p, kt_p, vt_p, u_p, kb_p, vtb_p, km_p = _in_projection(xp, g1, w_in_b, b_in2, tm_in, True, 0, 1, slen)
    km_p = km_p.reshape(bsz, slen // MOBA_BLOCK, ATTN_WIDTH)
    attn_p = _moba_prompt(q_p, kb_p, vtb_p, km_p, bsz, slen)
    y_p = _post_mixer_grouped(xp, attn_p, u_p, conv_w[l], cb, lg, lb, w_out_b, g2, w_rt, b_rt, wg_f, wu_f, wd_f, gf,
                              slen)

    xs = x_sample.reshape(dbsz, D_MODEL)
    q_s, k_s, v_s, u_s = _in_projection(xs, g1, w_in_b, b_in2, dbsz, False, past_len, 0, dbsz)
    heads = lambda t: t.reshape(dbsz, N_HEADS, HEAD_DIM)
    ck2 = cache_k[l].transpose(0, 2, 3, 1).reshape(-1, page_rows)
    cv2 = cache_v[l].transpose(0, 2, 3, 1).reshape(-1, page_rows)
    pt_flat = page_table.reshape(-1)
    q_lanes = jnp.broadcast_to(q_s[:, :, None], (dbsz, ATTN_WIDTH, LANES))
    sel = _kmean_select(pt_flat, q_lanes, ck2, dbsz, n_pages)
    sel_flat = sel[:, :, :MOBA_TOPK].reshape(-1)
    attn_s = _moba_sample(pt_flat, sel_flat, heads(q_s), heads(k_s), heads(v_s), ck2, cv2, dbsz, n_pages)
    conv_s = _conv_sample(state_conv[l], u_s, conv_w[l], cb, lg, lb)
    y_s = _post_mixer(xs, attn_s.reshape(dbsz, ATTN_WIDTH).astype(BF16), conv_s, w_out_b, g2, w_rt, b_rt,
                      wg_f, wu_f, wd_f, gf, dbsz, "post_mixer_sample")

    hist = CONV_LEN - 1
    kv_p = lambda t: t.reshape(bsz, N_HEADS, HEAD_DIM, slen).transpose(0, 3, 1, 2)[None]
    kv_s = lambda t: t.reshape(1, dbsz, 1, N_HEADS, HEAD_DIM)
    new_conv_p = u_p.reshape(bsz, slen, CONV_CH)[:, slen - hist:][None]
    new_conv_s = jnp.concatenate([state_conv[l][:, 1:], u_s[:, None, :]], axis=1)[None]
    return (y_p.reshape(bsz, slen, D_MODEL), y_s.reshape(dbsz, 1, D_MODEL), kv_p(kt_p), kv_p(vt_p), new_conv_p,
            kv_s(k_s), kv_s(v_s), new_conv_s)
```
